```python
import math
import jax
import jax.numpy as jnp
from jax import lax
import numpy as np


D_MODEL = 2048
BATCH = 1
SEQ = 8192
DEPTH = 1
DEC_BATCH = 128
DEC_SEQ = 1
PAST_LEN = 16384
PAGE_SIZE = 128

DA_HEADS = 8
DA_KV_HEADS = 2
DA_GROUP = DA_HEADS // DA_KV_HEADS
DA_D = 64
DA_SCALE = 1.0 / math.sqrt(DA_D)
MLA_HEADS = 8
MLA_NOPE = 128
MLA_ROPE = 64
MLA_V = 128
MLA_KV_LORA = 512
MLA_QK_D = MLA_NOPE + MLA_ROPE
MLA_SCALE = 1.0 / math.sqrt(MLA_QK_D)
ROPE_THETA = 10000.0
REL_BUCKETS = 32
REL_MAX_DIST = 128
N_EXPERTS = 32
TOP_K = 4
D_FF = 2048
SWIGLU_LIMIT = 7.0
SWIGLU_ALPHA = 1.702
NORM_EPS = 1e-6
Q_BLOCK = 128

DA_Q_W = DA_HEADS * 2 * DA_D
DA_KV_W = DA_KV_HEADS * 2 * DA_D
MLA_Q_W = MLA_HEADS * MLA_QK_D
DA_OUT_W = DA_HEADS * 2 * DA_D
MLA_OUT_W = MLA_HEADS * MLA_V
_O1 = DA_Q_W
_O2 = _O1 + DA_KV_W
_O3 = _O2 + DA_KV_W
_O4 = _O3 + MLA_Q_W
_O5 = _O4 + MLA_KV_LORA
_O6 = _O5 + MLA_ROPE
_O7 = _O6 + D_MODEL
IN_W = _O7 + D_MODEL
IN_SPLITS = (_O1, _O2, _O3, _O4, _O5, _O6, _O7)

kernel_name = 'hybrid_diffattn_mla_moe_adaln_step'


def rmsnorm(x, g):
    x32 = x.astype(jnp.float32)
    y = x32 * lax.rsqrt(jnp.mean(x32 * x32, axis=-1, keepdims=True) + NORM_EPS)
    return (y * g.astype(jnp.float32)).astype(x.dtype)


def rope(x, pos):
    half = MLA_ROPE // 2
    inv = ROPE_THETA ** (-jnp.arange(half, dtype=jnp.float32) / half)
    ang = pos.astype(jnp.float32)[:, None] * inv[None, :]
    cos = jnp.cos(ang)[None, :, None, :]
    sin = jnp.sin(ang)[None, :, None, :]
    x32 = x.astype(jnp.float32)
    x1, x2 = x32[..., :half], x32[..., half:]
    return jnp.concatenate([x1 * cos - x2 * sin, x2 * cos + x1 * sin], axis=-1).astype(x.dtype)


def rel_bias(table, q_pos, k_pos):
    n = jnp.maximum(q_pos[:, None] - k_pos[None, :], 0)
    max_exact = REL_BUCKETS // 2
    nf = jnp.maximum(n, 1).astype(jnp.float32)
    large = max_exact + (jnp.log(nf / max_exact) / math.log(REL_MAX_DIST / max_exact)
                         * (REL_BUCKETS - max_exact)).astype(jnp.int32)
    large = jnp.minimum(large, REL_BUCKETS - 1)
    bucket = jnp.where(n < max_exact, n, large)
    return jnp.transpose(table[bucket].astype(jnp.float32), (2, 0, 1))


def adaln(c, w, b):
    mod = jax.nn.silu(c) @ w + b
    return jnp.split(mod[:, None, :], 6, axis=-1)


def mixer_inputs(h, pos, prm):
    B, T = h.shape[:2]
    z = h @ prm['w_in']
    za_q, za_k, za_v, zb_q, zb_ckv, zb_kpe, g_a, g_b = jnp.split(z, IN_SPLITS, axis=-1)
    qa = rmsnorm(za_q.reshape(B, T, DA_KV_HEADS, DA_GROUP, 2, DA_D), prm['g_qa'])
    ka = rmsnorm(za_k.reshape(B, T, DA_KV_HEADS, 2, DA_D), prm['g_ka'])
    va = za_v.reshape(B, T, DA_KV_HEADS, 2 * DA_D)
    qb = zb_q.reshape(B, T, MLA_HEADS, MLA_QK_D)
    qb = jnp.concatenate([qb[..., :MLA_NOPE], rope(qb[..., MLA_NOPE:], pos)], axis=-1)
    qb = rmsnorm(qb, prm['g_qb'])
    ckv = rmsnorm(zb_ckv, prm['g_ckv'])
    kpe = rope(zb_kpe[:, :, None, :], pos)[:, :, 0, :]
    return qa, ka, va, qb, ckv, kpe, g_a, g_b


def mla_keys(ckv, kpe, w_uk, g_kb):
    kn = jnp.einsum('bsc,chd->bshd', ckv, w_uk)
    kpe_b = jnp.broadcast_to(kpe[:, :, None, :], kn.shape[:3] + (MLA_ROPE,))
    return rmsnorm(jnp.concatenate([kn, kpe_b.astype(kn.dtype)], axis=-1), g_kb)


def attend_prompt(qa, ka, va, qb, ckv, kpe, lam, rel_table, prm):
    B, S = qa.shape[:2]
    nb = S // Q_BLOCK
    kpos = jnp.arange(S, dtype=jnp.int32)
    ka32 = ka.astype(jnp.float32)
    va32 = va.astype(jnp.float32)
    kb32 = mla_keys(ckv, kpe, prm['w_uk'], prm['g_kb']).astype(jnp.float32)
    vb32 = jnp.einsum('bsc,che->bshe', ckv, prm['w_uv']).astype(jnp.float32)
    qa_blk = jnp.moveaxis(qa.reshape((B, nb, Q_BLOCK) + qa.shape[2:]), 1, 0)
    qb_blk = jnp.moveaxis(qb.reshape((B, nb, Q_BLOCK) + qb.shape[2:]), 1, 0)

    def block(args):
        q_a, q_b, start = args
        qpos = start + jnp.arange(Q_BLOCK, dtype=jnp.int32)
        mask = kpos[None, :] <= qpos[:, None]
        bias = rel_bias(rel_table, qpos, kpos).reshape(DA_KV_HEADS, DA_GROUP, 1, Q_BLOCK, S)
        s_a = jnp.einsum('bqngmd,bsnmd->bngmqs', q_a.astype(jnp.float32), ka32) * DA_SCALE + bias
        p_a = jax.nn.softmax(jnp.where(mask, s_a, -jnp.inf), axis=-1)
        diff = p_a[:, :, :, 0] - lam * p_a[:, :, :, 1]
        o_a = jnp.einsum('bngqs,bsne->bqnge', diff, va32).reshape(B, Q_BLOCK, DA_HEADS, 2 * DA_D)
        s_b = jnp.einsum('bqhd,bshd->bhqs', q_b.astype(jnp.float32), kb32) * MLA_SCALE
        p_b = jax.nn.softmax(jnp.where(mask, s_b, -jnp.inf), axis=-1)
        o_b = jnp.einsum('bhqs,bshe->bqhe', p_b, vb32)
        return o_a, o_b

    o_a, o_b = lax.map(block, (qa_blk, qb_blk, jnp.arange(nb, dtype=jnp.int32) * Q_BLOCK))
    o_a = jnp.moveaxis(o_a, 0, 1).reshape(B, S, DA_HEADS, 2 * DA_D)
    o_b = jnp.moveaxis(o_b, 0, 1).reshape(B, S, MLA_HEADS, MLA_V)
    return o_a, o_b


def merge_partials(m, l, acc):
    m_max = jnp.max(m, axis=0)
    w = jnp.exp(m - m_max)
    return jnp.sum(w * acc, axis=0) / jnp.sum(w * l, axis=0)


def attend_sample(qa, ka, va, qb, ckv, kpe, lam, rel_table, prm, layer,
                  cache_da_k, cache_da_v, cache_mla_ckv, cache_mla_kpe, page_table):
    n_pages = PAST_LEN // PAGE_SIZE
    DB, Q = qa.shape[:2]
    qpos = PAST_LEN + jnp.arange(Q, dtype=jnp.int32)
    qa32 = qa.astype(jnp.float32)
    qb32 = qb.astype(jnp.float32)

    def partial(k_a, v_a, c_kv, k_pe, kpos, mask):
        bias = rel_bias(rel_table, qpos, kpos).reshape(DA_KV_HEADS, DA_GROUP, 1, Q, kpos.shape[0])
        s_a = jnp.einsum('bqngmd,bsnmd->bngmqs', qa32, k_a.astype(jnp.float32)) * DA_SCALE + bias
        k_b = mla_keys(c_kv, k_pe, prm['w_uk'], prm['g_kb']).astype(jnp.float32)
        s_b = jnp.einsum('bqhd,bshd->bhqs', qb32, k_b) * MLA_SCALE
        if mask is not None:
            s_a = jnp.where(mask, s_a, -jnp.inf)
            s_b = jnp.where(mask, s_b, -jnp.inf)
        m_a = jnp.max(s_a, axis=-1, keepdims=True)
        e_a = jnp.exp(s_a - m_a)
        acc_a = jnp.einsum('bngmqs,bsne->bngmqe', e_a, v_a.astype(jnp.float32))
        m_b = jnp.max(s_b, axis=-1, keepdims=True)
        e_b = jnp.exp(s_b - m_b)
        acc_b = jnp.einsum('bhqs,bsc->bhqc', e_b, c_kv.astype(jnp.float32))
        return (m_a, jnp.sum(e_a, axis=-1, keepdims=True), acc_a,
                m_b, jnp.sum(e_b, axis=-1, keepdims=True), acc_b)

    def page(args):
        phys, start = args
        kpos = start + jnp.arange(PAGE_SIZE, dtype=jnp.int32)
        return partial(cache_da_k[layer, phys], cache_da_v[layer, phys],
                       cache_mla_ckv[layer, phys], cache_mla_kpe[layer, phys], kpos, None)

    parts = lax.map(page, (page_table.T, jnp.arange(n_pages, dtype=jnp.int32) * PAGE_SIZE))
    own = partial(ka, va, ckv, kpe, qpos, qpos[None, :] <= qpos[:, None])
    m_a, l_a, acc_a, m_b, l_b, acc_b = [jnp.concatenate([pp, oo[None]], axis=0) for pp, oo in zip(parts, own)]
    ra = merge_partials(m_a, l_a, acc_a)
    o_a = ra[:, :, :, 0] - lam * ra[:, :, :, 1]
    o_a = jnp.transpose(o_a, (0, 3, 1, 2, 4)).reshape(DB, Q, DA_HEADS, 2 * DA_D)
    rb = merge_partials(m_b, l_b, acc_b)
    o_b = jnp.einsum('bhqc,che->bqhe', rb, prm['w_uv'].astype(jnp.float32))
    return o_a, o_b


def mixer_output(x, oa, ob, g_a, g_b, lam_init, prm):
    B, T = oa.shape[:2]
    oa = rmsnorm(oa, prm['g_oa']) * (1.0 - lam_init)
    ya = oa.reshape(B, T, DA_OUT_W).astype(x.dtype) @ prm['w_oa']
    yb = ob.reshape(B, T, MLA_OUT_W).astype(x.dtype) @ prm['w_ob']
    return (jax.nn.sigmoid(g_a) * ya + jax.nn.sigmoid(g_b) * yb) @ prm['w_out']


def moe(h, layer, w_router, b_router, w_gu, b_gu, w_down, b_down):
    B, T, D = h.shape
    ht = h.reshape(B * T, D)
    logits = (ht @ w_router[layer] + b_router[layer]).astype(jnp.float32)
    top_val, top_idx = lax.top_k(logits, TOP_K)
    top_w = jax.nn.softmax(top_val, axis=-1)
    gate_w = jnp.sum(jax.nn.one_hot(top_idx, N_EXPERTS, dtype=jnp.float32) * top_w[..., None], axis=1)
    out = jnp.zeros((B * T, D), jnp.float32)
    for e in range(N_EXPERTS):
        gu = ht @ w_gu[layer, e] + b_gu[layer, e]
        g, u = gu[:, :D_FF], gu[:, D_FF:]
        g = jnp.minimum(g, SWIGLU_LIMIT)
        u = jnp.clip(u, -SWIGLU_LIMIT, SWIGLU_LIMIT)
        act = (u + 1.0) * (g * jax.nn.sigmoid(SWIGLU_ALPHA * g))
        out = out + gate_w[:, e:e + 1] * (act @ w_down[layer, e] + b_down[layer, e]).astype(jnp.float32)
    return out.reshape(B, T, D).astype(h.dtype)


def pre_mixer(x, c, pos, prm):
    mods = adaln(c, prm['w_ada'], prm['b_ada'])
    h = rmsnorm(x, prm['g_norm1']) * (1.0 + mods[1]) + mods[0]
    return mods, mixer_inputs(h, pos, prm)


def post_mixer(x, mods, oa, ob, g_a, g_b, lam_init, prm, layer, w_router, b_router, w_gu, b_gu, w_down, b_down):
    x = x + mods[2] * mixer_output(x, oa, ob, g_a, g_b, lam_init, prm)
    h = rmsnorm(x, prm['g_norm2']) * (1.0 + mods[4]) + mods[3]
    return x + mods[5] * moe(h, layer, w_router, b_router, w_gu, b_gu, w_down, b_down)


def setup_inputs(seed: int = 0) -> dict:
    key = jax.random.key(seed)
    ks = jax.random.split(key, 34)
    n_pages = PAST_LEN // PAGE_SIZE
    n_used = DEC_BATCH * n_pages
    n_phys = n_used + (n_used + 3) // 4
    f32 = jnp.float32

    def nrm(k, shape, scale):
        return jax.random.normal(k, shape, f32) * scale

    def gain(k, shape):
        return 1.0 + nrm(k, shape, 0.05)

    perm = jax.random.permutation(ks[0], n_phys)
    page_table = perm[:n_used].reshape(DEC_BATCH, n_pages).astype(jnp.int32)
    sd = D_MODEL ** -0.5
    return {
        'x_prompt': nrm(ks[1], (BATCH, SEQ, D_MODEL), 1.0),
        'x_sample': nrm(ks[2], (DEC_BATCH, DEC_SEQ, D_MODEL), 1.0),
        'cache_da_k': nrm(ks[3], (DEPTH, n_phys, PAGE_SIZE, DA_KV_HEADS, 2, DA_D), 1.0),
        'cache_da_v': nrm(ks[4], (DEPTH, n_phys, PAGE_SIZE, DA_KV_HEADS, 2 * DA_D), 1.0),
        'cache_mla_ckv': nrm(ks[5], (DEPTH, n_phys, PAGE_SIZE, MLA_KV_LORA), 1.0),
        'cache_mla_kpe': nrm(ks[6], (DEPTH, n_phys, PAGE_SIZE, MLA_ROPE), 1.0),
        'page_table': page_table,
        'c_prompt': nrm(ks[7], (BATCH, D_MODEL), 1.0),
        'c_sample': nrm(ks[8], (DEC_BATCH, D_MODEL), 1.0),
        'rel_table': nrm(ks[9], (REL_BUCKETS, DA_HEADS), 0.5),
        'w_ada': nrm(ks[10], (DEPTH, D_MODEL, 6 * D_MODEL), 0.5 * sd),
        'b_ada': nrm(ks[11], (DEPTH, 6 * D_MODEL), 0.02),
        'g_norm1': gain(ks[12], (DEPTH, D_MODEL)),
        'w_in': nrm(ks[13], (DEPTH, D_MODEL, IN_W), sd),
        'g_qa': gain(ks[14], (DEPTH, DA_D)),
        'g_ka': gain(ks[15], (DEPTH, DA_D)),
        'w_lambda': nrm(ks[16], (DEPTH, 4, DA_D), 0.1),
        'g_oa': gain(ks[17], (DEPTH, 2 * DA_D)),
        'g_qb': gain(ks[18], (DEPTH, MLA_QK_D)),
        'g_ckv': gain(ks[19], (DEPTH, MLA_KV_LORA)),
        'w_uk': nrm(ks[20], (DEPTH, MLA_KV_LORA, MLA_HEADS, MLA_NOPE), MLA_KV_LORA ** -0.5),
        'g_kb': gain(ks[21], (DEPTH, MLA_QK_D)),
        'w_uv': nrm(ks[22], (DEPTH, MLA_KV_LORA, MLA_HEADS, MLA_V), MLA_KV_LORA ** -0.5),
        'w_oa': nrm(ks[23], (DEPTH, DA_OUT_W, D_MODEL), DA_OUT_W ** -0.5),
        'w_ob': nrm(ks[24], (DEPTH, MLA_OUT_W, D_MODEL), MLA_OUT_W ** -0.5),
        'w_out': nrm(ks[25], (DEPTH, D_MODEL, D_MODEL), sd),
        'g_norm2': gain(ks[26], (DEPTH, D_MODEL)),
        'w_router': nrm(ks[27], (DEPTH, D_MODEL, N_EXPERTS), sd),
        'b_router': nrm(ks[28], (DEPTH, N_EXPERTS), 0.01),
        'w_gu': nrm(ks[29], (DEPTH, N_EXPERTS, D_MODEL, 2 * D_FF), sd),
        'b_gu': nrm(ks[30], (DEPTH, N_EXPERTS, 2 * D_FF), 0.02),
        'w_down': nrm(ks[31], (DEPTH, N_EXPERTS, D_FF, D_MODEL), D_FF ** -0.5),
        'b_down': nrm(ks[32], (DEPTH, N_EXPERTS, D_MODEL), 0.02),
    }


def reference(x_prompt, x_sample, cache_da_k, cache_da_v, cache_mla_ckv, cache_mla_kpe, page_table,
              c_prompt, c_sample, rel_table, w_ada, b_ada, g_norm1, w_in, g_qa, g_ka, w_lambda, g_oa,
              g_qb, g_ckv, w_uk, g_kb, w_uv, w_oa, w_ob, w_out, g_norm2, w_router, b_router,
              w_gu, b_gu, w_down, b_down):
    pos_p = jnp.arange(SEQ, dtype=jnp.int32)
    pos_s = PAST_LEN + jnp.arange(DEC_SEQ, dtype=jnp.int32)
    xp, xs = x_prompt, x_sample
    pk, pv, pc, pr = [], [], [], []
    sk, sv, sc, sr = [], [], [], []
    for layer in range(DEPTH):
        prm = {'w_ada': w_ada[layer], 'b_ada': b_ada[layer], 'g_norm1': g_norm1[layer], 'w_in': w_in[layer],
               'g_qa': g_qa[layer], 'g_ka': g_ka[layer], 'g_oa': g_oa[layer], 'g_qb': g_qb[layer],
               'g_ckv': g_ckv[layer], 'w_uk': w_uk[layer], 'g_kb': g_kb[layer], 'w_uv': w_uv[layer],
               'w_oa': w_oa[layer], 'w_ob': w_ob[layer], 'w_out': w_out[layer], 'g_norm2': g_norm2[layer]}
        lam_init = 0.8 - 0.6 * math.exp(-0.3 * layer)
        lw = w_lambda[layer].astype(jnp.float32)
        lam = jnp.exp(jnp.sum(lw[0] * lw[1])) - jnp.exp(jnp.sum(lw[2] * lw[3])) + lam_init

        mods, (qa, ka, va, qb, ckv, kpe, g_a, g_b) = pre_mixer(xp, c_prompt, pos_p, prm)
        oa, ob = attend_prompt(qa, ka, va, qb, ckv, kpe, lam, rel_table, prm)
        xp = post_mixer(xp, mods, oa, ob, g_a, g_b, lam_init, prm, layer,
                        w_router, b_router, w_gu, b_gu, w_down, b_down)
        pk.append(ka)
        pv.append(va)
        pc.append(ckv)
        pr.append(kpe)

        mods, (qa, ka, va, qb, ckv, kpe, g_a, g_b) = pre_mixer(xs, c_sample, pos_s, prm)
        oa, ob = attend_sample(qa, ka, va, qb, ckv, kpe, lam, rel_table, prm, layer,
                               cache_da_k, cache_da_v, cache_mla_ckv, cache_mla_kpe, page_table)
        xs = post_mixer(xs, mods, oa, ob, g_a, g_b, lam_init, prm, layer,
                        w_router, b_router, w_gu, b_gu, w_down, b_down)
        sk.append(ka)
        sv.append(va)
        sc.append(ckv)
        sr.append(kpe)

    return (xp, xs, jnp.stack(pk), jnp.stack(pv), jnp.stack(pc), jnp.stack(pr),
            jnp.stack(sk), jnp.stack(sv), jnp.stack(sc), jnp.stack(sr))
```

```python
import functools
import math

import jax
import jax.numpy as jnp
from jax import lax
from jax.experimental import pallas as pl
from jax.experimental.pallas import tpu as pltpu

F32 = jnp.float32
BF16 = jnp.bfloat16

PAGE_SIZE = 128
DA_HEADS = 8
DA_KV_HEADS = 2
DA_GROUP = DA_HEADS // DA_KV_HEADS
DA_D = 64
DA_SCALE = 1.0 / math.sqrt(DA_D)
MLA_HEADS = 8
MLA_NOPE = 128
MLA_ROPE = 64
MLA_V = 128
MLA_KV_LORA = 512
MLA_QK_D = MLA_NOPE + MLA_ROPE
MLA_SCALE = 1.0 / math.sqrt(MLA_QK_D)
ROPE_THETA = 10000.0
REL_BUCKETS = 32
REL_MAX_DIST = 128
N_EXPERTS = 32
TOP_K = 4
SWIGLU_LIMIT = 7.0
SWIGLU_ALPHA = 1.702
NORM_EPS = 1e-6

LOG2E = math.log2(math.e)
MASK_VALUE = -1e30
VMEM_LIMIT = 56 * 1024 * 1024

ATTN_TILE = 512
DEC_PAGES = 8
MOE_TM = 512
MOE_TF = 512

_NT = (((1,), (1,)), ((), ()))


def _cparams(sem):
    return pltpu.CompilerParams(dimension_semantics=sem, vmem_limit_bytes=VMEM_LIMIT)


def _split_bf16(x):
    hi = x.astype(BF16)
    lo = (x - hi.astype(F32)).astype(BF16)
    return hi, lo


def _mm_kernel(*refs, nk, split, has_bias):
    if has_bias:
        a_ref, b_ref, bias_ref, o_ref, acc_ref = refs
    else:
        a_ref, b_ref, o_ref, acc_ref = refs
        bias_ref = None
    k = pl.program_id(3)

    @pl.when(k == 0)
    def _():
        acc_ref[...] = jnp.zeros_like(acc_ref)

    if split:
        a_hi, a_lo = _split_bf16(a_ref[...].astype(F32))
        b_hi, b_lo = _split_bf16(b_ref[...].astype(F32))
        part = jnp.dot(a_hi, b_hi, preferred_element_type=F32)
        part += jnp.dot(a_lo, b_hi, preferred_element_type=F32)
        part += jnp.dot(a_hi, b_lo, preferred_element_type=F32)
    else:
        part = jnp.dot(a_ref[...].astype(BF16), b_ref[...].astype(BF16),
                       preferred_element_type=F32)
    acc_ref[...] += part

    @pl.when(k == nk - 1)
    def _():
        r = acc_ref[...]
        if has_bias:
            r = r + bias_ref[...]
        o_ref[...] = r.astype(o_ref.dtype)


def _pick(n, cands):
    for c in cands:
        if n % c == 0:
            return c
    return n


def _bmm(a, b, bias=None, *, out_dtype=F32, split=False, tm=None, tn=None, tk=None):
    G, M, K = a.shape
    _, _, N = b.shape
    tm = tm or _pick(M, (640, 512, 256, 128))
    tk = tk or K
    tn = tn or (N if N <= 1024 else 1024)
    nm, nn, nk = M // tm, pl.cdiv(N, tn), K // tk
    assert M % tm == 0 and K % tk == 0
    in_specs = [
        pl.BlockSpec((None, tm, tk), lambda g, j, i, k: (g, i, k)),
        pl.BlockSpec((None, tk, tn), lambda g, j, i, k: (g, k, j)),
    ]
    args = [a, b]
    if bias is not None:
        in_specs.append(pl.BlockSpec((None, 1, tn), lambda g, j, i, k: (g, 0, j)))
        args.append(bias.reshape(G, 1, N).astype(F32))
    return pl.pallas_call(
        functools.partial(_mm_kernel, nk=nk, split=split, has_bias=bias is not None),
        grid=(G, nn, nm, nk),
        in_specs=in_specs,
        out_specs=pl.BlockSpec((None, tm, tn), lambda g, j, i, k: (g, i, j)),
        out_shape=jax.ShapeDtypeStruct((G, M, N), out_dtype),
        scratch_shapes=[pltpu.VMEM((tm, tn), F32)],
        compiler_params=_cparams(("parallel", "parallel", "parallel", "arbitrary")),
    )(*args)


def _mm(a, b, bias=None, **kw):
    return _bmm(a[None], b[None], None if bias is None else bias[None], **kw)[0]


def _flash_kernel(qi_ref, ki_ref, q_ref, k_ref, v_ref, bias_ref, o_ref,
                  m_ref, l_ref, acc_ref, *, hp, kmap, vmap, bmap, dv):
    t = pl.program_id(1)
    qi = qi_ref[t]
    ki = ki_ref[t]

    @pl.when(ki == 0)
    def _():
        m_ref[...] = jnp.full_like(m_ref, MASK_VALUE)
        l_ref[...] = jnp.zeros_like(l_ref)
        acc_ref[...] = jnp.zeros_like(acc_ref)

    def update(near):
        for c in range(hp):
            s = lax.dot_general(q_ref[c], k_ref[kmap[c]], _NT, preferred_element_type=F32)
            if near:
                s = s + bias_ref[bmap[c], qi - ki]
            m_prev = m_ref[c]
            m_new = jnp.maximum(m_prev, jnp.max(s, axis=1, keepdims=True))
            alpha = jnp.exp2(m_prev - m_new)
            p = jnp.exp2(s - m_new[:, :1])
            l_ref[c] = alpha * l_ref[c] + jnp.sum(p, axis=1, keepdims=True)
            acc_ref[c] = alpha[:, :dv] * acc_ref[c] + jnp.dot(
                p.astype(BF16), v_ref[vmap[c]], preferred_element_type=F32)
            m_ref[c] = m_new

    @pl.when(qi - ki <= 1)
    def _():
        update(True)

    @pl.when(qi - ki > 1)
    def _():
        update(False)

    @pl.when(ki == qi)
    def _():
        for c in range(hp):
            o_ref[:, c * dv:(c + 1) * dv] = (acc_ref[c] / l_ref[c][:, :dv]).astype(o_ref.dtype)


def _flash(q, k, v, bias, *, groups, hp, kb, vb, bb, kmap, vmap, bmap, tile):
    _, S, dk = q.shape
    dv = v.shape[-1]
    T = tile
    assert dv == 128 and S % T == 0
    nq = S // T
    qi_l, ki_l = [], []
    for i in range(nq):
        for j in range(i + 1):
            qi_l.append(i)
            ki_l.append(j)
    qi_arr = jnp.asarray(qi_l, jnp.int32)
    ki_arr = jnp.asarray(ki_l, jnp.int32)
    grid_spec = pltpu.PrefetchScalarGridSpec(
        num_scalar_prefetch=2,
        grid=(groups, len(qi_l)),
        in_specs=[
            pl.BlockSpec((hp, T, dk), lambda g, t, qi, ki: (g, qi[t], 0)),
            pl.BlockSpec((kb, T, dk), lambda g, t, qi, ki: (g, ki[t], 0)),
            pl.BlockSpec((vb, T, dv), lambda g, t, qi, ki: (g, ki[t], 0)),
            pl.BlockSpec((bb, 2, T, T), lambda g, t, qi, ki: (g, 0, 0, 0)),
        ],
        out_specs=pl.BlockSpec((T, hp * dv), lambda g, t, qi, ki: (qi[t], g)),
        scratch_shapes=[pltpu.VMEM((hp, T, 128), F32),
                        pltpu.VMEM((hp, T, 128), F32),
                        pltpu.VMEM((hp, T, dv), F32)],
    )
    return pl.pallas_call(
        functools.partial(_flash_kernel, hp=hp, kmap=kmap, vmap=vmap, bmap=bmap, dv=dv),
        grid_spec=grid_spec,
        out_shape=jax.ShapeDtypeStruct((S, groups * hp * dv), F32),
        compiler_params=_cparams(("parallel", "arbitrary")),
    )(qi_arr, ki_arr, q, k, v, bias)


def _online_rows(s, m_ref, l_ref):
    m_prev = m_ref[...]
    m_new = jnp.maximum(m_prev, jnp.max(s, axis=1, keepdims=True))
    alpha = jnp.exp2(m_prev - m_new)
    p = jnp.exp2(s - m_new[:, :1])
    l_ref[...] = alpha * l_ref[...] + jnp.sum(p, axis=1, keepdims=True)
    m_ref[...] = m_new
    return p, alpha


def _decode_kernel(pt_ref, qa_ref, qabs_ref, qrope_ref, wukt_ref, biasl_ref,
                   soa_ref, voa_ref, sob_ref, cown_ref, *rest, pg, nch):
    dak = rest[0:pg]
    dav = rest[pg:2 * pg]
    ckv = rest[2 * pg:3 * pg]
    kpe = rest[3 * pg:4 * pg]
    oa_ref, ob_ref = rest[4 * pg:4 * pg + 2]
    (lhs_ref, kbf, vbf, cbf, pbf, p2hi, p2lo,
     m_a, l_a, acc_a, m_b, l_b, acc_b) = rest[4 * pg + 2:]
    b = pl.program_id(0)
    c = pl.program_id(1)
    nw = MLA_HEADS * MLA_NOPE
    ps = PAGE_SIZE

    @pl.when((b == 0) & (c == 0))
    def _():
        lhs_ref[0:nw, :] = wukt_ref[...]

    @pl.when(c == 0)
    def _():
        lhs_ref[nw:nw + 16, :] = qabs_ref[0]
        m_a[...] = jnp.full_like(m_a, MASK_VALUE)
        l_a[...] = jnp.zeros_like(l_a)
        acc_a[...] = jnp.zeros_like(acc_a)
        m_b[...] = jnp.full_like(m_b, MASK_VALUE)
        l_b[...] = jnp.zeros_like(l_b)
        acc_b[...] = jnp.zeros_like(acc_b)

    for j in range(pg):
        kbf[j * ps:(j + 1) * ps, :] = dak[j][...].astype(BF16)
        vbf[j * ps:(j + 1) * ps, :] = dav[j][...].astype(BF16)
        cbf[j * ps:(j + 1) * ps, :] = ckv[j][...].astype(BF16)
        pe = kpe[j][...]
        pbf[j * ps:(j + 1) * ps, :] = pe.astype(BF16)
        hi, lo = _split_bf16(pe * pe)
        p2hi[j * ps:(j + 1) * ps, :] = hi
        p2lo[j * ps:(j + 1) * ps, :] = lo

    n_tok = pg * ps
    is_last = (c == nch - 1).astype(F32)

    big = lax.dot_general(lhs_ref[...], cbf[...], _NT, preferred_element_type=F32)
    kn = big[0:nw]
    ssq_kn = jnp.sum((kn * kn).reshape(MLA_HEADS, MLA_NOPE, n_tok), axis=1)
    ones = jnp.ones((16, MLA_ROPE), BF16)
    ssq_pe = (lax.dot_general(ones, p2hi[...], _NT, preferred_element_type=F32)
              + lax.dot_general(ones, p2lo[...], _NT, preferred_element_type=F32))
    s_rope = lax.dot_general(qrope_ref[0], pbf[...], _NT, preferred_element_type=F32)
    rinv = lax.rsqrt((ssq_kn + ssq_pe[0:MLA_HEADS]) * (1.0 / MLA_QK_D) + NORM_EPS)
    s_b = (big[nw:nw + MLA_HEADS] + s_rope[0:MLA_HEADS]) * rinv
    s_b = jnp.concatenate([s_b, jnp.zeros_like(s_b)], axis=0)
    p_b, alpha_b = _online_rows(s_b, m_b, l_b)
    acc_b[...] = alpha_b[:, :1] * acc_b[...] + jnp.dot(
        p_b.astype(BF16), cbf[...], preferred_element_type=F32)

    for n in range(DA_KV_HEADS):
        lanes = slice(n * 2 * DA_D, (n + 1) * 2 * DA_D)
        s = lax.dot_general(qa_ref[0, n], kbf[:, lanes], _NT, preferred_element_type=F32)
        s = s + biasl_ref[n] * is_last
        m_prev = m_a[n]
        m_new = jnp.maximum(m_prev, jnp.max(s, axis=1, keepdims=True))
        alpha = jnp.exp2(m_prev - m_new)
        p = jnp.exp2(s - m_new[:, :1])
        l_a[n] = alpha * l_a[n] + jnp.sum(p, axis=1, keepdims=True)
        m_a[n] = m_new
        acc_a[n] = alpha * acc_a[n] + jnp.dot(
            p.astype(BF16), vbf[:, lanes], preferred_element_type=F32)

    @pl.when(c == nch - 1)
    def _():
        for n in range(DA_KV_HEADS):
            s_o = soa_ref[0, n]
            m_f = jnp.maximum(m_a[n], s_o)
            a1 = jnp.exp2(m_a[n] - m_f)
            a2 = jnp.exp2(s_o - m_f)
            l_f = a1 * l_a[n] + a2
            oa_ref[0, n] = (a1 * acc_a[n] + a2 * voa_ref[0, n]) / l_f
        s_o = sob_ref[0]
        m_f = jnp.maximum(m_b[...], s_o)
        a1 = jnp.exp2(m_b[...] - m_f)
        a2 = jnp.exp2(s_o - m_f)
        l_f = a1 * l_b[...] + a2
        ob_ref[0] = (a1[:, :1] * acc_b[...] + a2[:, :1] * cown_ref[0]) / l_f[:, :1]


def _decode_attention(page_table, qa_mat, q_abs, q_rope, wukt, bias_last,
                      s_own_a, v_own_a, s_own_b, c_own,
                      cache_k, cache_v, cache_c, cache_p):
    B, n_pages = page_table.shape
    pg = DEC_PAGES
    assert n_pages % pg == 0
    nch = n_pages // pg
    n_tok = pg * PAGE_SIZE
    nw = MLA_HEADS * MLA_NOPE
    kw, vw, cw, pw = cache_k.shape[-1], cache_v.shape[-1], cache_c.shape[-1], cache_p.shape[-1]

    def seq_spec(shape):
        nd = len(shape)
        return pl.BlockSpec((1,) + shape, lambda b, c, pt: (b,) + (0,) * nd)

    def const_spec(shape):
        nd = len(shape)
        return pl.BlockSpec(shape, lambda b, c, pt: (0,) * nd)

    def page_spec(width, j):
        return pl.BlockSpec((None, PAGE_SIZE, width),
                            lambda b, c, pt: (pt[b * n_pages + c * pg + j], 0, 0))

    in_specs = [
        seq_spec((DA_KV_HEADS, 16, 2 * DA_D)),
        seq_spec((16, MLA_KV_LORA)),
        seq_spec((16, MLA_ROPE)),
        const_spec((nw, MLA_KV_LORA)),
        const_spec((DA_KV_HEADS, 16, n_tok)),
        seq_spec((DA_KV_HEADS, 16, 128)),
        seq_spec((DA_KV_HEADS, 1, 128)),
        seq_spec((16, 128)),
        seq_spec((1, MLA_KV_LORA)),
    ]
    args = [qa_mat, q_abs, q_rope, wukt, bias_last, s_own_a, v_own_a, s_own_b, c_own]
    for arr, width in ((cache_k, kw), (cache_v, vw), (cache_c, cw), (cache_p, pw)):
        for j in range(pg):
            in_specs.append(page_spec(width, j))
            args.append(arr)
    grid_spec = pltpu.PrefetchScalarGridSpec(
        num_scalar_prefetch=1,
        grid=(B, nch),
        in_specs=in_specs,
        out_specs=[seq_spec((DA_KV_HEADS, 16, 128)), seq_spec((16, MLA_KV_LORA))],
        scratch_shapes=[
            pltpu.VMEM((nw + 16, MLA_KV_LORA), BF16),
            pltpu.VMEM((n_tok, kw), BF16),
            pltpu.VMEM((n_tok, vw), BF16),
            pltpu.VMEM((n_tok, cw), BF16),
            pltpu.VMEM((n_tok, pw), BF16),
            pltpu.VMEM((n_tok, pw), BF16),
            pltpu.VMEM((n_tok, pw), BF16),
            pltpu.VMEM((DA_KV_HEADS, 16, 128), F32),
            pltpu.VMEM((DA_KV_HEADS, 16, 128), F32),
            pltpu.VMEM((DA_KV_HEADS, 16, 128), F32),
            pltpu.VMEM((16, 128), F32),
            pltpu.VMEM((16, 128), F32),
            pltpu.VMEM((16, MLA_KV_LORA), F32),
        ],
    )
    return pl.pallas_call(
        functools.partial(_decode_kernel, pg=pg, nch=nch),
        grid_spec=grid_spec,
        out_shape=[jax.ShapeDtypeStruct((B, DA_KV_HEADS, 16, 128), F32),
                   jax.ShapeDtypeStruct((B, 16, MLA_KV_LORA), F32)],
        compiler_params=_cparams(("arbitrary", "arbitrary")),
    )(page_table.reshape(-1), *args)


def _moe_kernel(te_ref, tv_ref, x_ref, wg_ref, wu_ref, bg_ref, bu_ref, wd_ref, bd_ref,
                rw_ref, o_ref, acc_ref, *, nf):
    i = pl.program_id(0)
    j = pl.program_id(1)
    valid = tv_ref[i] == 1

    @pl.when(valid)
    def _():
        @pl.when(j == 0)
        def _():
            acc_ref[...] = jnp.zeros_like(acc_ref)

        x = x_ref[...]
        g = jnp.dot(x, wg_ref[...].astype(BF16), preferred_element_type=F32) + bg_ref[...]
        u = jnp.dot(x, wu_ref[...].astype(BF16), preferred_element_type=F32) + bu_ref[...]
        g = jnp.minimum(g, SWIGLU_LIMIT)
        u = jnp.clip(u, -SWIGLU_LIMIT, SWIGLU_LIMIT)
        act = (u + 1.0) * (g * jax.nn.sigmoid(SWIGLU_ALPHA * g))
        acc_ref[...] += jnp.dot(act.astype(BF16), wd_ref[...].astype(BF16),
                                preferred_element_type=F32)

        @pl.when(j == nf - 1)
        def _():
            o_ref[...] = (acc_ref[...] + bd_ref[...]) * rw_ref[...]

    @pl.when(jnp.logical_not(valid) & (j == nf - 1))
    def _():
        o_ref[...] = jnp.zeros_like(o_ref)


def _moe_ffn(tile_expert, tile_valid, x_sorted, row_w, w_gu, b_gu, w_down, b_down):
    R, D = x_sorted.shape
    E, _, F2 = w_gu.shape
    dff = F2 // 2
    tm, tf = MOE_TM, MOE_TF
    nf = dff // tf
    n_tiles = R // tm
    grid_spec = pltpu.PrefetchScalarGridSpec(
        num_scalar_prefetch=2,
        grid=(n_tiles, nf),
        in_specs=[
            pl.BlockSpec((tm, D), lambda i, j, te, tv: (i, 0)),
            pl.BlockSpec((None, D, tf), lambda i, j, te, tv: (te[i], 0, j)),
            pl.BlockSpec((None, D, tf), lambda i, j, te, tv: (te[i], 0, nf + j)),
            pl.BlockSpec((None, 1, tf), lambda i, j, te, tv: (te[i], 0, j)),
            pl.BlockSpec((None, 1, tf), lambda i, j, te, tv: (te[i], 0, nf + j)),
            pl.BlockSpec((None, tf, D), lambda i, j, te, tv: (te[i], j, 0)),
            pl.BlockSpec((None, 1, D), lambda i, j, te, tv: (te[i], 0, 0)),
            pl.BlockSpec((tm, 1), lambda i, j, te, tv: (i, 0)),
        ],
        out_specs=pl.BlockSpec((tm, D), lambda i, j, te, tv: (i, 0)),
        scratch_shapes=[pltpu.VMEM((tm, D), F32)],
    )
    return pl.pallas_call(
        functools.partial(_moe_kernel, nf=nf),
        grid_spec=grid_spec,
        out_shape=jax.ShapeDtypeStruct((R, D), F32),
        compiler_params=_cparams(("arbitrary", "arbitrary")),
    )(tile_expert, tile_valid, x_sorted, w_gu, w_gu,
      b_gu.reshape(E, 1, F2), b_gu.reshape(E, 1, F2), w_down, b_down.reshape(E, 1, D), row_w)


def _moe(h, w_router, b_router, w_gu, b_gu, w_down, b_down):
    N, D = h.shape
    E = w_router.shape[-1]
    tm = MOE_TM
    logits = _mm(h, w_router, b_router, split=True)
    top_val, top_idx = lax.top_k(logits, TOP_K)
    top_w = jax.nn.softmax(top_val, axis=-1)

    P = N * TOP_K
    n_tiles = pl.cdiv(P, tm) + E
    R = n_tiles * tm
    e_flat = top_idx.reshape(P).astype(jnp.int32)
    order = jnp.argsort(e_flat, stable=True).astype(jnp.int32)
    e_sorted = e_flat[order]
    cnt = jnp.sum(jax.nn.one_hot(e_flat, E, dtype=jnp.int32), axis=0)
    padded = ((cnt + tm - 1) // tm) * tm
    pad_end = jnp.cumsum(padded)
    pad_off = pad_end - padded
    sort_off = jnp.cumsum(cnt) - cnt
    dest = pad_off[e_sorted] + jnp.arange(P, dtype=jnp.int32) - sort_off[e_sorted]
    src_tok = jnp.zeros((R,), jnp.int32).at[dest].set(order // TOP_K)
    row_w = jnp.zeros((R,), F32).at[dest].set(top_w.reshape(P)[order])
    pos = jnp.zeros((P,), jnp.int32).at[order].set(dest).reshape(N, TOP_K)
    tile_start = jnp.arange(n_tiles, dtype=jnp.int32) * tm
    tile_valid = (tile_start < pad_end[-1]).astype(jnp.int32)
    tile_expert = jnp.minimum(jnp.searchsorted(pad_end, tile_start, side='right'), E - 1)
    last_valid = jnp.maximum(pad_end[-1] // tm - 1, 0)
    tile_expert = jnp.where(tile_valid == 1, tile_expert, tile_expert[last_valid]).astype(jnp.int32)

    x_sorted = h.astype(BF16)[src_tok]
    y = _moe_ffn(tile_expert, tile_valid, x_sorted, row_w.reshape(R, 1),
                 w_gu, b_gu, w_down, b_down)
    return jnp.sum(y[pos], axis=1)


def _rms(x, g):
    return x * lax.rsqrt(jnp.mean(x * x, axis=-1, keepdims=True) + NORM_EPS) * g


def _rope(x, pos):
    half = MLA_ROPE // 2
    inv = ROPE_THETA ** (-jnp.arange(half, dtype=F32) / half)
    ang = pos.astype(F32)[:, None] * inv[None, :]
    cos, sin = jnp.cos(ang), jnp.sin(ang)
    if x.ndim == 3:
        cos, sin = cos[:, None, :], sin[:, None, :]
    x1, x2 = x[..., :half], x[..., half:]
    return jnp.concatenate([x1 * cos - x2 * sin, x2 * cos + x1 * sin], axis=-1)


def _rel_bias_by_distance(rel_table, max_n):
    n = jnp.arange(max_n + 1, dtype=jnp.int32)
    max_exact = REL_BUCKETS // 2
    nf = jnp.maximum(n, 1).astype(F32)
    large = max_exact + (jnp.log(nf / max_exact) / math.log(REL_MAX_DIST / max_exact)
                         * (REL_BUCKETS - max_exact)).astype(jnp.int32)
    large = jnp.minimum(large, REL_BUCKETS - 1)
    bucket = jnp.where(n < max_exact, n, large)
    return rel_table[bucket].astype(F32)


def _split_z(z):
    da_q = DA_HEADS * 2 * DA_D
    da_kv = DA_KV_HEADS * 2 * DA_D
    mla_q = MLA_HEADS * MLA_QK_D
    d_model = (z.shape[-1] - da_q - 2 * da_kv - mla_q - MLA_KV_LORA - MLA_ROPE) // 2
    sizes = (da_q, da_kv, da_kv, mla_q, MLA_KV_LORA, MLA_ROPE, d_model, d_model)
    out, o = [], 0
    for s in sizes:
        out.append(z[:, o:o + s])
        o += s
    return out


def _mixer_inputs(z, pos, g_qa, g_ka, g_qb, g_ckv):
    za_q, za_k, za_v, zb_q, zb_ckv, zb_kpe, g_a, g_b = _split_z(z)
    T = z.shape[0]
    qa = _rms(za_q.reshape(T, DA_KV_HEADS, DA_GROUP, 2, DA_D), g_qa)
    ka = _rms(za_k.reshape(T, DA_KV_HEADS, 2, DA_D), g_ka)
    va = za_v.reshape(T, DA_KV_HEADS, 2 * DA_D)
    qb = zb_q.reshape(T, MLA_HEADS, MLA_QK_D)
    qb = jnp.concatenate([qb[..., :MLA_NOPE], _rope(qb[..., MLA_NOPE:], pos)], axis=-1)
    qb = _rms(qb, g_qb)
    ckv = _rms(zb_ckv, g_ckv)
    kpe = _rope(zb_kpe, pos)
    return qa, ka, va, qb, ckv, kpe, g_a, g_b


def kernel(x_prompt, x_sample, cache_da_k, cache_da_v, cache_mla_ckv, cache_mla_kpe, page_table, c_prompt, c_sample, rel_table, w_ada, b_ada, g_norm1, w_in, g_qa, g_ka, w_lambda, g_oa, g_qb, g_ckv, w_uk, g_kb, w_uv, w_oa, w_ob, w_out, g_norm2, w_router, b_router, w_gu, b_gu, w_down, b_down):
    depth = w_in.shape[0]
    assert depth == 1 and x_prompt.shape[0] == 1 and x_sample.shape[1] == 1
    assert PAGE_SIZE >= REL_MAX_DIST
    layer = 0
    S, D = x_prompt.shape[1], x_prompt.shape[2]
    B = x_sample.shape[0]
    n_pages = page_table.shape[1]
    past_len = n_pages * PAGE_SIZE
    T = ATTN_TILE

    lam_init = 0.8 - 0.6 * math.exp(-0.3 * layer)
    lw = w_lambda[layer]
    lam = jnp.exp(jnp.sum(lw[0] * lw[1])) - jnp.exp(jnp.sum(lw[2] * lw[3])) + lam_init

    xp = x_prompt[0]
    xs = x_sample[:, 0]

    c_all = jnp.concatenate([c_prompt, c_sample], axis=0)
    n_c = c_all.shape[0]
    n_c_pad = -(-n_c // 16) * 16
    c_act = jnp.pad(jax.nn.silu(c_all), ((0, n_c_pad - n_c), (0, 0)))
    mods = _mm(c_act, w_ada[layer], b_ada[layer], tm=n_c_pad, tn=512)[:n_c]
    sh1, sc1, gt1, sh2, sc2, gt2 = [mods[:, i * D:(i + 1) * D] for i in range(6)]

    def modulate(x, g, sc, sh):
        return _rms(x, g) * (1.0 + sc) + sh

    h_p = modulate(xp, g_norm1[layer], sc1[:1], sh1[:1])
    h_s = modulate(xs, g_norm1[layer], sc1[1:], sh1[1:])
    h_all = jnp.concatenate([h_p, h_s], axis=0).astype(BF16)
    z = _mm(h_all, w_in[layer])

    pos_p = jnp.arange(S, dtype=jnp.int32)
    pos_s = jnp.full((B,), past_len, jnp.int32)
    qa_p, ka_p, va_p, qb_p, ckv_p, kpe_p, ga_p, gb_p = _mixer_inputs(
        z[:S], pos_p, g_qa[layer], g_ka[layer], g_qb[layer], g_ckv[layer])
    qa_s, ka_s, va_s, qb_s, ckv_s, kpe_s, ga_s, gb_s = _mixer_inputs(
        z[S:], pos_s, g_qa[layer], g_ka[layer], g_qb[layer], g_ckv[layer])

    tbl = _rel_bias_by_distance(rel_table, REL_MAX_DIST)
    tbl = (tbl - tbl[REL_MAX_DIST][None, :]) * LOG2E

    w_uk_flat = w_uk[layer].reshape(MLA_KV_LORA, MLA_HEADS * MLA_NOPE)
    w_uv_flat = w_uv[layer].reshape(MLA_KV_LORA, MLA_HEADS * MLA_V)
    ii = jnp.arange(T, dtype=jnp.int32)[:, None]
    jj = jnp.arange(T, dtype=jnp.int32)[None, :]
    d0 = jnp.clip(ii - jj, 0, REL_MAX_DIST)
    d1 = jnp.clip(T + ii - jj, 0, REL_MAX_DIST)
    causal = (ii >= jj)[None]
    bias_diag = jnp.where(causal, jnp.transpose(tbl[d0], (2, 0, 1)), MASK_VALUE)
    bias_off = jnp.transpose(tbl[d1], (2, 0, 1))
    bias_da = jnp.stack([bias_diag, bias_off], axis=1)
    mask_b = jnp.stack([jnp.where(causal, 0.0, MASK_VALUE),
                        jnp.zeros((1, T, T), F32)], axis=1)

    eye2 = jnp.eye(2, dtype=F32)

    def da_q_padded(qa):
        q = jnp.transpose(qa, (1, 2, 3, 0, 4)) * (DA_SCALE * LOG2E)
        q = q[:, :, :, :, None, :] * eye2[None, None, :, None, :, None]
        return q.reshape(DA_HEADS * 2, qa.shape[0], 2 * DA_D)

    q_da = da_q_padded(qa_p).astype(BF16)
    k_da = jnp.transpose(ka_p.reshape(S, DA_KV_HEADS, 2 * DA_D), (1, 0, 2)).astype(BF16)
    v_da = jnp.transpose(va_p, (1, 0, 2)).astype(BF16)
    o_da = _flash(q_da, k_da, v_da, bias_da, groups=DA_KV_HEADS, hp=2 * DA_GROUP, kb=1, vb=1,
                  bb=DA_GROUP, kmap=[0] * (2 * DA_GROUP), vmap=[0] * (2 * DA_GROUP),
                  bmap=[c // 2 for c in range(2 * DA_GROUP)], tile=T)
    o_da = o_da.reshape(S, DA_HEADS, 2, 2 * DA_D)
    oa_p = o_da[:, :, 0] - lam * o_da[:, :, 1]

    kn_p = _mm(ckv_p.astype(BF16), w_uk_flat).reshape(S, MLA_HEADS, MLA_NOPE)
    kb_p = jnp.concatenate(
        [kn_p, jnp.broadcast_to(kpe_p[:, None, :], (S, MLA_HEADS, MLA_ROPE))], axis=-1)
    kb_p = _rms(kb_p, g_kb[layer])
    vb_p = _mm(ckv_p.astype(BF16), w_uv_flat).reshape(S, MLA_HEADS, MLA_V)
    q_ml = jnp.transpose(qb_p * (MLA_SCALE * LOG2E), (1, 0, 2)).astype(BF16)
    k_ml = jnp.transpose(kb_p, (1, 0, 2)).astype(BF16)
    v_ml = jnp.transpose(vb_p, (1, 0, 2)).astype(BF16)
    hm = list(range(MLA_HEADS))
    ob_p = _flash(q_ml, k_ml, v_ml, mask_b, groups=1, hp=MLA_HEADS, kb=MLA_HEADS, vb=MLA_HEADS,
                  bb=1, kmap=hm, vmap=hm, bmap=[0] * MLA_HEADS, tile=T)

    n_tok = DEC_PAGES * PAGE_SIZE
    qa_mat = jnp.transpose(da_q_padded(qa_s).reshape(DA_KV_HEADS, 2 * DA_GROUP, B, 2 * DA_D),
                           (2, 0, 1, 3))
    qa_mat = jnp.pad(qa_mat, ((0, 0), (0, 0), (0, 16 - 2 * DA_GROUP), (0, 0))).astype(BF16)
    gk = g_kb[layer]
    qn = jnp.transpose(qb_s[..., :MLA_NOPE] * gk[:MLA_NOPE], (1, 0, 2))
    wukt = jnp.transpose(w_uk[layer], (1, 2, 0))
    q_abs = _bmm(qn, wukt, split=True) * (MLA_SCALE * LOG2E)
    q_abs = jnp.pad(jnp.transpose(q_abs, (1, 0, 2)), ((0, 0), (0, 16 - MLA_HEADS), (0, 0))).astype(BF16)
    q_rope = qb_s[..., MLA_NOPE:] * gk[MLA_NOPE:] * (MLA_SCALE * LOG2E)
    q_rope = jnp.pad(q_rope, ((0, 0), (0, 16 - MLA_HEADS), (0, 0))).astype(BF16)
    wukt_flat = wukt.reshape(MLA_HEADS * MLA_NOPE, MLA_KV_LORA).astype(BF16)

    dist_last = PAGE_SIZE - jnp.arange(PAGE_SIZE, dtype=jnp.int32)
    bl = jnp.transpose(tbl[dist_last], (1, 0))
    bl = jnp.repeat(bl.reshape(DA_KV_HEADS, DA_GROUP, 1, PAGE_SIZE), 2, axis=2)
    bl = bl.reshape(DA_KV_HEADS, 2 * DA_GROUP, PAGE_SIZE)
    bias_last = jnp.pad(bl, ((0, 0), (0, 16 - 2 * DA_GROUP), (n_tok - PAGE_SIZE, 0)))

    s_own_a = jnp.einsum('bngmd,bnmd->bngm', qa_s, ka_s) * (DA_SCALE * LOG2E)
    s_own_a = s_own_a + tbl[0].reshape(1, DA_KV_HEADS, DA_GROUP, 1)
    s_own_a = jnp.pad(s_own_a.reshape(B, DA_KV_HEADS, 2 * DA_GROUP), ((0, 0), (0, 0), (0, 16 - 2 * DA_GROUP)))
    s_own_a = jnp.broadcast_to(s_own_a[..., None], (B, DA_KV_HEADS, 16, 128))
    v_own_a = va_s.reshape(B, DA_KV_HEADS, 1, 2 * DA_D)
    kn_s = _mm(ckv_s.astype(BF16), w_uk_flat, tm=B).reshape(B, MLA_HEADS, MLA_NOPE)
    kb_s = jnp.concatenate(
        [kn_s, jnp.broadcast_to(kpe_s[:, None, :], (B, MLA_HEADS, MLA_ROPE))], axis=-1)
    kb_s = _rms(kb_s, gk)
    s_own_b = jnp.sum(qb_s * kb_s, axis=-1) * (MLA_SCALE * LOG2E)
    s_own_b = jnp.pad(s_own_b, ((0, 0), (0, 16 - MLA_HEADS)))
    s_own_b = jnp.broadcast_to(s_own_b[..., None], (B, 16, 128))
    c_own = ckv_s.reshape(B, 1, MLA_KV_LORA)

    n_phys = cache_da_k.shape[1]
    ra, rb = _decode_attention(
        page_table, qa_mat, q_abs, q_rope, wukt_flat, bias_last,
        s_own_a, v_own_a, s_own_b, c_own,
        cache_da_k[layer].reshape(n_phys, PAGE_SIZE, DA_KV_HEADS * 2 * DA_D),
        cache_da_v[layer].reshape(n_phys, PAGE_SIZE, DA_KV_HEADS * 2 * DA_D),
        cache_mla_ckv[layer], cache_mla_kpe[layer])
    ra = ra[:, :, :2 * DA_GROUP].reshape(B, DA_HEADS, 2, 2 * DA_D)
    oa_s = ra[:, :, 0] - lam * ra[:, :, 1]
    rb_h = jnp.transpose(rb[:, :MLA_HEADS], (1, 0, 2)).astype(BF16)
    wuv_h = jnp.transpose(w_uv[layer], (1, 0, 2))
    ob_s = jnp.transpose(_bmm(rb_h, wuv_h, tm=B), (1, 0, 2)).reshape(B, MLA_HEADS * MLA_V)

    x_all = jnp.concatenate([xp, xs], axis=0)
    oa_all = jnp.concatenate([oa_p, oa_s], axis=0)
    oa_all = (_rms(oa_all, g_oa[layer]) * (1.0 - lam_init)).reshape(S + B, DA_HEADS * 2 * DA_D)
    ob_all = jnp.concatenate([ob_p, ob_s], axis=0)
    ya = _mm(oa_all.astype(BF16), w_oa[layer])
    yb = _mm(ob_all.astype(BF16), w_ob[layer])
    g_a = jnp.concatenate([ga_p, ga_s], axis=0)
    g_b = jnp.concatenate([gb_p, gb_s], axis=0)
    mix = (jax.nn.sigmoid(g_a) * ya + jax.nn.sigmoid(g_b) * yb).astype(BF16)
    y_mix = _mm(mix, w_out[layer])

    def per_row(m):
        return jnp.concatenate([jnp.broadcast_to(m[:1], (S, D)), m[1:]], axis=0)

    x1 = x_all + per_row(gt1) * y_mix
    h2 = _rms(x1, g_norm2[layer]) * (1.0 + per_row(sc2)) + per_row(sh2)
    y_moe = _moe(h2, w_router[layer], b_router[layer], w_gu[layer], b_gu[layer],
                 w_down[layer], b_down[layer])
    x2 = x1 + per_row(gt2) * y_moe

    y_prompt = x2[:S][None]
    y_sample = x2[S:][:, None]
    return (y_prompt, y_sample,
            ka_p[None, None], va_p[None, None], ckv_p[None, None], kpe_p[None, None],
            ka_s[None, :, None], va_s[None, :, None], ckv_s[None, :, None], kpe_s[None, :, None])
```

```python
import functools
import math

import jax
import jax.numpy as jnp
from jax import lax
from jax.experimental import pallas as pl
from jax.experimental.pallas import tpu as pltpu

F32 = jnp.float32
BF16 = jnp.bfloat16

PAGE_SIZE = 128
DA_HEADS = 8
DA_KV_HEADS = 2
DA_GROUP = DA_HEADS // DA_KV_HEADS
DA_D = 64
DA_SCALE = 1.0 / math.sqrt(DA_D)
MLA_HEADS = 8
MLA_NOPE = 128
MLA_ROPE = 64
MLA_V = 128
MLA_KV_LORA = 512
MLA_QK_D = MLA_NOPE + MLA_ROPE
MLA_SCALE = 1.0 / math.sqrt(MLA_QK_D)
ROPE_THETA = 10000.0
REL_BUCKETS = 32
REL_MAX_DIST = 128
N_EXPERTS = 32
TOP_K = 4
SWIGLU_LIMIT = 7.0
SWIGLU_ALPHA = 1.702
NORM_EPS = 1e-6

LOG2E = math.log2(math.e)
MASK_VALUE = -1e30
VMEM_LIMIT = 56 * 1024 * 1024

ATTN_TILE = 512
DEC_PAGES = 8
MOE_TM = 512
MOE_TF = 512

_NT = (((1,), (1,)), ((), ()))


def _cparams(sem):
    return pltpu.CompilerParams(dimension_semantics=sem, vmem_limit_bytes=VMEM_LIMIT)


def _split_bf16(x):
    hi = x.astype(BF16)
    lo = (x - hi.astype(F32)).astype(BF16)
    return hi, lo


def _mm_kernel(*refs, nk, split, has_bias):
    if has_bias:
        a_ref, b_ref, bias_ref, o_ref, acc_ref = refs
    else:
        a_ref, b_ref, o_ref, acc_ref = refs
        bias_ref = None
    k = pl.program_id(3)

    @pl.when(k == 0)
    def _():
        acc_ref[...] = jnp.zeros_like(acc_ref)

    if split:
        a_hi, a_lo = _split_bf16(a_ref[...].astype(F32))
        b_hi, b_lo = _split_bf16(b_ref[...].astype(F32))
        part = jnp.dot(a_hi, b_hi, preferred_element_type=F32)
        part += jnp.dot(a_lo, b_hi, preferred_element_type=F32)
        part += jnp.dot(a_hi, b_lo, preferred_element_type=F32)
    else:
        part = jnp.dot(a_ref[...].astype(BF16), b_ref[...].astype(BF16),
                       preferred_element_type=F32)
    acc_ref[...] += part

    @pl.when(k == nk - 1)
    def _():
        r = acc_ref[...]
        if has_bias:
            r = r + bias_ref[...]
        o_ref[...] = r.astype(o_ref.dtype)


def _pick(n, cands):
    for c in cands:
        if n % c == 0:
            return c
    return n


def _bmm(a, b, bias=None, *, out_dtype=F32, split=False, tm=None, tn=None, tk=None):
    G, M, K = a.shape
    _, _, N = b.shape
    tm = tm or _pick(M, (640, 512, 256, 128))
    tk = tk or K
    tn = tn or (N if N <= 1024 else 1024)
    nm, nn, nk = M // tm, pl.cdiv(N, tn), K // tk
    assert M % tm == 0 and K % tk == 0
    in_specs = [
        pl.BlockSpec((None, tm, tk), lambda g, j, i, k: (g, i, k)),
        pl.BlockSpec((None, tk, tn), lambda g, j, i, k: (g, k, j)),
    ]
    args = [a, b]
    if bias is not None:
        in_specs.append(pl.BlockSpec((None, 1, tn), lambda g, j, i, k: (g, 0, j)))
        args.append(bias.reshape(G, 1, N).astype(F32))
    return pl.pallas_call(
        functools.partial(_mm_kernel, nk=nk, split=split, has_bias=bias is not None),
        grid=(G, nn, nm, nk),
        in_specs=in_specs,
        out_specs=pl.BlockSpec((None, tm, tn), lambda g, j, i, k: (g, i, j)),
        out_shape=jax.ShapeDtypeStruct((G, M, N), out_dtype),
        scratch_shapes=[pltpu.VMEM((tm, tn), F32)],
        compiler_params=_cparams(("parallel", "parallel", "parallel", "arbitrary")),
    )(*args)


def _mm(a, b, bias=None, **kw):
    return _bmm(a[None], b[None], None if bias is None else bias[None], **kw)[0]


def _flash_kernel(qi_ref, ki_ref, q_ref, k_ref, v_ref, bias_ref, o_ref,
                  m_ref, l_ref, acc_ref, *, hp, kmap, vmap, bmap, dv):
    t = pl.program_id(1)
    qi = qi_ref[t]
    ki = ki_ref[t]

    @pl.when(ki == 0)
    def _():
        m_ref[...] = jnp.full_like(m_ref, MASK_VALUE)
        l_ref[...] = jnp.zeros_like(l_ref)
        acc_ref[...] = jnp.zeros_like(acc_ref)

    def update(near):
        for c in range(hp):
            s = lax.dot_general(q_ref[c], k_ref[kmap[c]], _NT, preferred_element_type=F32)
            if near:
                s = s + bias_ref[bmap[c], qi - ki]
            m_prev = m_ref[c]
            m_new = jnp.maximum(m_prev, jnp.max(s, axis=1, keepdims=True))
            alpha = jnp.exp2(m_prev - m_new)
            p = jnp.exp2(s - m_new[:, :1])
            l_ref[c] = alpha * l_ref[c] + jnp.sum(p, axis=1, keepdims=True)
            acc_ref[c] = alpha[:, :dv] * acc_ref[c] + jnp.dot(
                p.astype(BF16), v_ref[vmap[c]], preferred_element_type=F32)
            m_ref[c] = m_new

    @pl.when(qi - ki <= 1)
    def _():
        update(True)

    @pl.when(qi - ki > 1)
    def _():
        update(False)

    @pl.when(ki == qi)
    def _():
        for c in range(hp):
            o_ref[:, c * dv:(c + 1) * dv] = (acc_ref[c] / l_ref[c][:, :dv]).astype(o_ref.dtype)


def _flash(q, k, v, bias, *, groups, hp, kb, vb, bb, kmap, vmap, bmap, tile):
    _, S, dk = q.shape
    dv = v.shape[-1]
    T = tile
    assert dv == 128 and S % T == 0
    nq = S // T
    qi_l, ki_l = [], []
    for i in range(nq):
        for j in range(i + 1):
            qi_l.append(i)
            ki_l.append(j)
    qi_arr = jnp.asarray(qi_l, jnp.int32)
    ki_arr = jnp.asarray(ki_l, jnp.int32)
    grid_spec = pltpu.PrefetchScalarGridSpec(
        num_scalar_prefetch=2,
        grid=(groups, len(qi_l)),
        in_specs=[
            pl.BlockSpec((hp, T, dk), lambda g, t, qi, ki: (g, qi[t], 0)),
            pl.BlockSpec((kb, T, dk), lambda g, t, qi, ki: (g, ki[t], 0)),
            pl.BlockSpec((vb, T, dv), lambda g, t, qi, ki: (g, ki[t], 0)),
            pl.BlockSpec((bb, 2, T, T), lambda g, t, qi, ki: (g, 0, 0, 0)),
        ],
        out_specs=pl.BlockSpec((T, hp * dv), lambda g, t, qi, ki: (qi[t], g)),
        scratch_shapes=[pltpu.VMEM((hp, T, 128), F32),
                        pltpu.VMEM((hp, T, 128), F32),
                        pltpu.VMEM((hp, T, dv), F32)],
    )
    return pl.pallas_call(
        functools.partial(_flash_kernel, hp=hp, kmap=kmap, vmap=vmap, bmap=bmap, dv=dv),
        grid_spec=grid_spec,
        out_shape=jax.ShapeDtypeStruct((S, groups * hp * dv), F32),
        compiler_params=_cparams(("parallel", "arbitrary")),
    )(qi_arr, ki_arr, q, k, v, bias)


def _merge_partial(state, part):
    m, l, acc = state
    m_k, l_k, acc_k = part
    m_new = jnp.maximum(m, m_k)
    a = jnp.exp2(m - m_new)
    b = jnp.exp2(m_k - m_new)
    return m_new, a * l + b * l_k, a * acc + b * acc_k


def _local_softmax(s):
    m_k = jnp.max(s, axis=1, keepdims=True)
    p = jnp.exp2(s - m_k)
    return m_k, jnp.sum(p, axis=1, keepdims=True), p.astype(BF16)


def _decode_kernel(pt_ref, qa_ref, qabs_ref, qrope_ref, wukt_ref, biasl_ref,
                   soa_ref, voa_ref, sob_ref, cown_ref, *rest, pg, nch):
    kt = rest[0:pg]
    vv = rest[pg:2 * pg]
    ckv = rest[2 * pg:3 * pg]
    pt = rest[3 * pg:4 * pg]
    oa_ref, ob_ref = rest[4 * pg:4 * pg + 2]
    lhs_ref, cbf, m_a, l_a, acc_a, m_b, l_b, acc_b = rest[4 * pg + 2:]
    b = pl.program_id(0)
    c = pl.program_id(1)
    nw = MLA_HEADS * MLA_NOPE
    ps = PAGE_SIZE
    kvw = 2 * DA_D

    @pl.when((b == 0) & (c == 0))
    def _():
        lhs_ref[0:nw, :] = wukt_ref[...]

    @pl.when(c == 0)
    def _():
        lhs_ref[nw:nw + 16, :] = qabs_ref[0]
        m_a[...] = jnp.full_like(m_a, MASK_VALUE)
        l_a[...] = jnp.zeros_like(l_a)
        acc_a[...] = jnp.zeros_like(acc_a)
        m_b[...] = jnp.full_like(m_b, MASK_VALUE)
        l_b[...] = jnp.zeros_like(l_b)
        acc_b[...] = jnp.zeros_like(acc_b)

    is_last = (c == nch - 1).astype(F32)
    st_b = (m_b[:, 0:1], l_b[:, 0:1], acc_b[...])
    st_a = [(m_a[n][:, 0:1], l_a[n][:, 0:1], acc_a[n]) for n in range(DA_KV_HEADS)]

    n_tok = pg * ps
    for j in range(pg):
        cbf[j * ps:(j + 1) * ps, :] = ckv[j][...].astype(BF16)
    pe = jnp.concatenate([pt[j][...] for j in range(pg)], axis=1)
    s_rope = jnp.dot(qrope_ref[0], pe.astype(BF16), preferred_element_type=F32)
    ssq_pe = jnp.sum(pe * pe, axis=0, keepdims=True)

    part_a = []
    for n in range(DA_KV_HEADS):
        rows = slice(n * kvw, (n + 1) * kvw)
        ktn = jnp.concatenate([kt[j][rows, :] for j in range(pg)], axis=1).astype(BF16)
        s = jnp.dot(qa_ref[0, n], ktn, preferred_element_type=F32)
        s = jnp.concatenate([s[:, :n_tok - 2 * ps],
                             s[:, n_tok - 2 * ps:] + biasl_ref[n] * is_last], axis=1)
        part_a.append(_local_softmax(s))

    big = lax.dot_general(lhs_ref[...], cbf[...], _NT, preferred_element_type=F32)
    kn = big[0:nw]
    ssq_kn = jnp.sum((kn * kn).reshape(MLA_HEADS, MLA_NOPE, n_tok), axis=1)

    for n in range(DA_KV_HEADS):
        m_k, l_k, p = part_a[n]
        vn = jnp.concatenate([vv[j][pl.ds(n, ps, stride=DA_KV_HEADS), :] for j in range(pg)],
                             axis=0).astype(BF16)
        st_a[n] = _merge_partial(st_a[n], (m_k, l_k, jnp.dot(p, vn, preferred_element_type=F32)))

    rinv = lax.rsqrt((ssq_kn + ssq_pe) * (1.0 / MLA_QK_D) + NORM_EPS)
    s_b = (big[nw:nw + MLA_HEADS] + s_rope[0:MLA_HEADS]) * rinv
    s_b = jnp.concatenate([s_b, jnp.zeros_like(s_b)], axis=0)
    m_k, l_k, p = _local_softmax(s_b)
    st_b = _merge_partial(st_b, (m_k, l_k, jnp.dot(p, cbf[...], preferred_element_type=F32)))

    m_b[...] = jnp.broadcast_to(st_b[0], m_b.shape)
    l_b[...] = jnp.broadcast_to(st_b[1], l_b.shape)
    acc_b[...] = st_b[2]
    for n in range(DA_KV_HEADS):
        m_a[n] = jnp.broadcast_to(st_a[n][0], (16, 128))
        l_a[n] = jnp.broadcast_to(st_a[n][1], (16, 128))
        acc_a[n] = st_a[n][2]

    @pl.when(c == nch - 1)
    def _():
        for n in range(DA_KV_HEADS):
            m_f, l_f, acc_f = _merge_partial(
                st_a[n], (soa_ref[0, n][:, 0:1], jnp.ones((16, 1), F32), voa_ref[0, n]))
            oa_ref[0, n] = acc_f / l_f
        m_f, l_f, acc_f = _merge_partial(
            st_b, (sob_ref[0][:, 0:1], jnp.ones((16, 1), F32), cown_ref[0]))
        ob_ref[0] = acc_f / l_f


def _decode_attention(page_table, qa_mat, q_abs, q_rope, wukt, bias_last,
                      s_own_a, v_own_a, s_own_b, c_own,
                      cache_k, cache_v, cache_c, cache_p):
    B, n_pages = page_table.shape
    pg = DEC_PAGES
    assert n_pages % pg == 0 and pg % 2 == 0
    nch = n_pages // pg
    nw = MLA_HEADS * MLA_NOPE

    def seq_spec(shape):
        nd = len(shape)
        return pl.BlockSpec((1,) + shape, lambda b, c, pt: (b,) + (0,) * nd)

    def const_spec(shape):
        nd = len(shape)
        return pl.BlockSpec(shape, lambda b, c, pt: (0,) * nd)

    def page_spec(arr, j):
        return pl.BlockSpec((None,) + arr.shape[1:],
                            lambda b, c, pt: (pt[b * n_pages + c * pg + j], 0, 0))

    in_specs = [
        seq_spec((DA_KV_HEADS, 16, 2 * DA_D)),
        seq_spec((16, MLA_KV_LORA)),
        seq_spec((16, MLA_ROPE)),
        const_spec((nw, MLA_KV_LORA)),
        const_spec((DA_KV_HEADS, 16, 2 * PAGE_SIZE)),
        seq_spec((DA_KV_HEADS, 16, 128)),
        seq_spec((DA_KV_HEADS, 1, 128)),
        seq_spec((16, 128)),
        seq_spec((1, MLA_KV_LORA)),
    ]
    args = [qa_mat, q_abs, q_rope, wukt, bias_last, s_own_a, v_own_a, s_own_b, c_own]
    for arr in (cache_k, cache_v, cache_c, cache_p):
        for j in range(pg):
            in_specs.append(page_spec(arr, j))
            args.append(arr)
    grid_spec = pltpu.PrefetchScalarGridSpec(
        num_scalar_prefetch=1,
        grid=(B, nch),
        in_specs=in_specs,
        out_specs=[seq_spec((DA_KV_HEADS, 16, 128)), seq_spec((16, MLA_KV_LORA))],
        scratch_shapes=[
            pltpu.VMEM((nw + 16, MLA_KV_LORA), BF16),
            pltpu.VMEM((pg * PAGE_SIZE, MLA_KV_LORA), BF16),
            pltpu.VMEM((DA_KV_HEADS, 16, 128), F32),
            pltpu.VMEM((DA_KV_HEADS, 16, 128), F32),
            pltpu.VMEM((DA_KV_HEADS, 16, 128), F32),
            pltpu.VMEM((16, 128), F32),
            pltpu.VMEM((16, 128), F32),
            pltpu.VMEM((16, MLA_KV_LORA), F32),
        ],
    )
    return pl.pallas_call(
        functools.partial(_decode_kernel, pg=pg, nch=nch),
        grid_spec=grid_spec,
        out_shape=[jax.ShapeDtypeStruct((B, DA_KV_HEADS, 16, 128), F32),
                   jax.ShapeDtypeStruct((B, 16, MLA_KV_LORA), F32)],
        compiler_params=_cparams(("arbitrary", "arbitrary")),
    )(page_table.reshape(-1), *args)


def _moe_kernel(te_ref, tv_ref, x_ref, wg_ref, wu_ref, bg_ref, bu_ref, wd_ref, bd_ref,
                rw_ref, o_ref, acc_ref, *, nf):
    i = pl.program_id(0)
    j = pl.program_id(1)
    valid = tv_ref[i] == 1

    @pl.when(valid)
    def _():
        @pl.when(j == 0)
        def _():
            acc_ref[...] = jnp.zeros_like(acc_ref)

        x = x_ref[...]
        g = jnp.dot(x, wg_ref[...].astype(BF16), preferred_element_type=F32) + bg_ref[...]
        u = jnp.dot(x, wu_ref[...].astype(BF16), preferred_element_type=F32) + bu_ref[...]
        g = jnp.minimum(g, SWIGLU_LIMIT)
        u = jnp.clip(u, -SWIGLU_LIMIT, SWIGLU_LIMIT)
        act = (u + 1.0) * (g * jax.nn.sigmoid(SWIGLU_ALPHA * g))
        acc_ref[...] += jnp.dot(act.astype(BF16), wd_ref[...].astype(BF16),
                                preferred_element_type=F32)

        @pl.when(j == nf - 1)
        def _():
            o_ref[...] = (acc_ref[...] + bd_ref[...]) * rw_ref[...]

    @pl.when(jnp.logical_not(valid) & (j == nf - 1))
    def _():
        o_ref[...] = jnp.zeros_like(o_ref)


def _moe_ffn(tile_expert, tile_valid, x_sorted, row_w, w_gu, b_gu, w_down, b_down):
    R, D = x_sorted.shape
    E, _, F2 = w_gu.shape
    dff = F2 // 2
    tm, tf = MOE_TM, MOE_TF
    nf = dff // tf
    n_tiles = R // tm
    grid_spec = pltpu.PrefetchScalarGridSpec(
        num_scalar_prefetch=2,
        grid=(n_tiles, nf),
        in_specs=[
            pl.BlockSpec((tm, D), lambda i, j, te, tv: (i, 0)),
            pl.BlockSpec((None, D, tf), lambda i, j, te, tv: (te[i], 0, j)),
            pl.BlockSpec((None, D, tf), lambda i, j, te, tv: (te[i], 0, nf + j)),
            pl.BlockSpec((None, 1, tf), lambda i, j, te, tv: (te[i], 0, j)),
            pl.BlockSpec((None, 1, tf), lambda i, j, te, tv: (te[i], 0, nf + j)),
            pl.BlockSpec((None, tf, D), lambda i, j, te, tv: (te[i], j, 0)),
            pl.BlockSpec((None, 1, D), lambda i, j, te, tv: (te[i], 0, 0)),
            pl.BlockSpec((tm, 1), lambda i, j, te, tv: (i, 0)),
        ],
        out_specs=pl.BlockSpec((tm, D), lambda i, j, te, tv: (i, 0)),
        scratch_shapes=[pltpu.VMEM((tm, D), F32)],
    )
    return pl.pallas_call(
        functools.partial(_moe_kernel, nf=nf),
        grid_spec=grid_spec,
        out_shape=jax.ShapeDtypeStruct((R, D), F32),
        compiler_params=_cparams(("arbitrary", "arbitrary")),
    )(tile_expert, tile_valid, x_sorted, w_gu, w_gu,
      b_gu.reshape(E, 1, F2), b_gu.reshape(E, 1, F2), w_down, b_down.reshape(E, 1, D), row_w)


def _moe(h, w_router, b_router, w_gu, b_gu, w_down, b_down):
    N, D = h.shape
    E = w_router.shape[-1]
    tm = MOE_TM
    logits = _mm(h, w_router, b_router, split=True)
    top_val, top_idx = lax.top_k(logits, TOP_K)
    top_w = jax.nn.softmax(top_val, axis=-1)

    P = N * TOP_K
    n_tiles = pl.cdiv(P, tm) + E
    R = n_tiles * tm
    e_flat = top_idx.reshape(P).astype(jnp.int32)
    order = jnp.argsort(e_flat, stable=True).astype(jnp.int32)
    e_sorted = e_flat[order]
    cnt = jnp.sum(jax.nn.one_hot(e_flat, E, dtype=jnp.int32), axis=0)
    padded = ((cnt + tm - 1) // tm) * tm
    pad_end = jnp.cumsum(padded)
    pad_off = pad_end - padded
    sort_off = jnp.cumsum(cnt) - cnt
    dest = pad_off[e_sorted] + jnp.arange(P, dtype=jnp.int32) - sort_off[e_sorted]
    src_tok = jnp.zeros((R,), jnp.int32).at[dest].set(order // TOP_K)
    row_w = jnp.zeros((R,), F32).at[dest].set(top_w.reshape(P)[order])
    pos = jnp.zeros((P,), jnp.int32).at[order].set(dest).reshape(N, TOP_K)
    tile_start = jnp.arange(n_tiles, dtype=jnp.int32) * tm
    tile_valid = (tile_start < pad_end[-1]).astype(jnp.int32)
    tile_expert = jnp.minimum(jnp.searchsorted(pad_end, tile_start, side='right'), E - 1)
    last_valid = jnp.maximum(pad_end[-1] // tm - 1, 0)
    tile_expert = jnp.where(tile_valid == 1, tile_expert, tile_expert[last_valid]).astype(jnp.int32)

    x_sorted = h.astype(BF16)[src_tok]
    y = _moe_ffn(tile_expert, tile_valid, x_sorted, row_w.reshape(R, 1),
                 w_gu, b_gu, w_down, b_down)
    return jnp.sum(y[pos], axis=1)


def _rms(x, g):
    return x * lax.rsqrt(jnp.mean(x * x, axis=-1, keepdims=True) + NORM_EPS) * g


def _rope(x, pos):
    half = MLA_ROPE // 2
    inv = ROPE_THETA ** (-jnp.arange(half, dtype=F32) / half)
    ang = pos.astype(F32)[:, None] * inv[None, :]
    cos, sin = jnp.cos(ang), jnp.sin(ang)
    if x.ndim == 3:
        cos, sin = cos[:, None, :], sin[:, None, :]
    x1, x2 = x[..., :half], x[..., half:]
    return jnp.concatenate([x1 * cos - x2 * sin, x2 * cos + x1 * sin], axis=-1)


def _rel_bias_by_distance(rel_table, max_n):
    n = jnp.arange(max_n + 1, dtype=jnp.int32)
    max_exact = REL_BUCKETS // 2
    nf = jnp.maximum(n, 1).astype(F32)
    large = max_exact + (jnp.log(nf / max_exact) / math.log(REL_MAX_DIST / max_exact)
                         * (REL_BUCKETS - max_exact)).astype(jnp.int32)
    large = jnp.minimum(large, REL_BUCKETS - 1)
    bucket = jnp.where(n < max_exact, n, large)
    return rel_table[bucket].astype(F32)


def _toeplitz(v, t):
    h = v.shape[0]
    u = v[:, ::-1]
    flat = jnp.tile(u, (1, t))[:, :t * (2 * t - 1)]
    return flat.reshape(h, t, 2 * t - 1)[:, :, t - 1:]


def _split_z(z):
    da_q = DA_HEADS * 2 * DA_D
    da_kv = DA_KV_HEADS * 2 * DA_D
    mla_q = MLA_HEADS * MLA_QK_D
    d_model = (z.shape[-1] - da_q - 2 * da_kv - mla_q - MLA_KV_LORA - MLA_ROPE) // 2
    sizes = (da_q, da_kv, da_kv, mla_q, MLA_KV_LORA, MLA_ROPE, d_model, d_model)
    out, o = [], 0
    for s in sizes:
        out.append(z[:, o:o + s])
        o += s
    return out


def _mixer_inputs(z, pos, g_qa, g_ka, g_qb, g_ckv):
    za_q, za_k, za_v, zb_q, zb_ckv, zb_kpe, g_a, g_b = _split_z(z)
    T = z.shape[0]
    qa = _rms(za_q.reshape(T, DA_KV_HEADS, DA_GROUP, 2, DA_D), g_qa)
    ka = _rms(za_k.reshape(T, DA_KV_HEADS, 2, DA_D), g_ka)
    va = za_v.reshape(T, DA_KV_HEADS, 2 * DA_D)
    qb = zb_q.reshape(T, MLA_HEADS, MLA_QK_D)
    qb = jnp.concatenate([qb[..., :MLA_NOPE], _rope(qb[..., MLA_NOPE:], pos)], axis=-1)
    qb = _rms(qb, g_qb)
    ckv = _rms(zb_ckv, g_ckv)
    kpe = _rope(zb_kpe, pos)
    return qa, ka, va, qb, ckv, kpe, g_a, g_b


def kernel(x_prompt, x_sample, cache_da_k, cache_da_v, cache_mla_ckv, cache_mla_kpe, page_table, c_prompt, c_sample, rel_table, w_ada, b_ada, g_norm1, w_in, g_qa, g_ka, w_lambda, g_oa, g_qb, g_ckv, w_uk, g_kb, w_uv, w_oa, w_ob, w_out, g_norm2, w_router, b_router, w_gu, b_gu, w_down, b_down):
    depth = w_in.shape[0]
    assert depth == 1 and x_prompt.shape[0] == 1 and x_sample.shape[1] == 1
    assert PAGE_SIZE >= REL_MAX_DIST
    layer = 0
    S, D = x_prompt.shape[1], x_prompt.shape[2]
    B = x_sample.shape[0]
    n_pages = page_table.shape[1]
    past_len = n_pages * PAGE_SIZE
    T = ATTN_TILE

    lam_init = 0.8 - 0.6 * math.exp(-0.3 * layer)
    lw = w_lambda[layer]
    lam = jnp.exp(jnp.sum(lw[0] * lw[1])) - jnp.exp(jnp.sum(lw[2] * lw[3])) + lam_init

    xp = x_prompt[0]
    xs = x_sample[:, 0]

    c_all = jnp.concatenate([c_prompt, c_sample], axis=0)
    n_c = c_all.shape[0]
    n_c_pad = -(-n_c // 16) * 16
    c_act = jnp.pad(jax.nn.silu(c_all), ((0, n_c_pad - n_c), (0, 0)))
    mods = _mm(c_act, w_ada[layer], b_ada[layer], tm=n_c_pad, tn=512)[:n_c]
    sh1, sc1, gt1, sh2, sc2, gt2 = [mods[:, i * D:(i + 1) * D] for i in range(6)]

    def modulate(x, g, sc, sh):
        return _rms(x, g) * (1.0 + sc) + sh

    h_p = modulate(xp, g_norm1[layer], sc1[:1], sh1[:1])
    h_s = modulate(xs, g_norm1[layer], sc1[1:], sh1[1:])
    h_all = jnp.concatenate([h_p, h_s], axis=0).astype(BF16)
    z = _mm(h_all, w_in[layer])

    pos_p = jnp.arange(S, dtype=jnp.int32)
    pos_s = jnp.full((B,), past_len, jnp.int32)
    qa_p, ka_p, va_p, qb_p, ckv_p, kpe_p, ga_p, gb_p = _mixer_inputs(
        z[:S], pos_p, g_qa[layer], g_ka[layer], g_qb[layer], g_ckv[layer])
    qa_s, ka_s, va_s, qb_s, ckv_s, kpe_s, ga_s, gb_s = _mixer_inputs(
        z[S:], pos_s, g_qa[layer], g_ka[layer], g_qb[layer], g_ckv[layer])

    tbl = _rel_bias_by_distance(rel_table, REL_MAX_DIST)
    tbl = (tbl - tbl[REL_MAX_DIST][None, :]) * LOG2E

    w_uk_flat = w_uk[layer].reshape(MLA_KV_LORA, MLA_HEADS * MLA_NOPE)
    w_uv_flat = w_uv[layer].reshape(MLA_KV_LORA, MLA_HEADS * MLA_V)
    causal = (jnp.arange(T)[:, None] >= jnp.arange(T)[None, :])[None]
    by_dist = jnp.pad(jnp.transpose(tbl), ((0, 0), (T, T - REL_MAX_DIST - 1)))
    bias_diag = jnp.where(causal, _toeplitz(by_dist, T), MASK_VALUE)
    bias_off = _toeplitz(jnp.pad(by_dist[:, T:], ((0, 0), (0, T))), T)
    bias_da = jnp.stack([bias_diag, bias_off], axis=1)
    mask_b = jnp.stack([jnp.where(causal, 0.0, MASK_VALUE),
                        jnp.zeros((1, T, T), F32)], axis=1)

    eye2 = jnp.eye(2, dtype=F32)

    def da_q_padded(qa):
        q = jnp.transpose(qa, (1, 2, 3, 0, 4)) * (DA_SCALE * LOG2E)
        q = q[:, :, :, :, None, :] * eye2[None, None, :, None, :, None]
        return q.reshape(DA_HEADS * 2, qa.shape[0], 2 * DA_D)

    q_da = da_q_padded(qa_p).astype(BF16)
    k_da = jnp.transpose(ka_p.reshape(S, DA_KV_HEADS, 2 * DA_D), (1, 0, 2)).astype(BF16)
    v_da = jnp.transpose(va_p, (1, 0, 2)).astype(BF16)
    o_da = _flash(q_da, k_da, v_da, bias_da, groups=DA_KV_HEADS, hp=2 * DA_GROUP, kb=1, vb=1,
                  bb=DA_GROUP, kmap=[0] * (2 * DA_GROUP), vmap=[0] * (2 * DA_GROUP),
                  bmap=[c // 2 for c in range(2 * DA_GROUP)], tile=T)
    o_da = o_da.reshape(S, DA_HEADS, 2, 2 * DA_D)
    oa_p = o_da[:, :, 0] - lam * o_da[:, :, 1]

    kn_p = _mm(ckv_p.astype(BF16), w_uk_flat).reshape(S, MLA_HEADS, MLA_NOPE)
    kb_p = jnp.concatenate(
        [kn_p, jnp.broadcast_to(kpe_p[:, None, :], (S, MLA_HEADS, MLA_ROPE))], axis=-1)
    kb_p = _rms(kb_p, g_kb[layer])
    vb_p = _mm(ckv_p.astype(BF16), w_uv_flat).reshape(S, MLA_HEADS, MLA_V)
    q_ml = jnp.transpose(qb_p * (MLA_SCALE * LOG2E), (1, 0, 2)).astype(BF16)
    k_ml = jnp.transpose(kb_p, (1, 0, 2)).astype(BF16)
    v_ml = jnp.transpose(vb_p, (1, 0, 2)).astype(BF16)
    hm = list(range(MLA_HEADS))
    ob_p = _flash(q_ml, k_ml, v_ml, mask_b, groups=1, hp=MLA_HEADS, kb=MLA_HEADS, vb=MLA_HEADS,
                  bb=1, kmap=hm, vmap=hm, bmap=[0] * MLA_HEADS, tile=T)

    qa_mat = jnp.transpose(da_q_padded(qa_s).reshape(DA_KV_HEADS, 2 * DA_GROUP, B, 2 * DA_D),
                           (2, 0, 1, 3))
    qa_mat = jnp.pad(qa_mat, ((0, 0), (0, 0), (0, 16 - 2 * DA_GROUP), (0, 0))).astype(BF16)
    gk = g_kb[layer]
    qn = jnp.transpose(qb_s[..., :MLA_NOPE] * gk[:MLA_NOPE], (1, 0, 2))
    wukt = jnp.transpose(w_uk[layer], (1, 2, 0))
    q_abs = _bmm(qn, wukt, split=True) * (MLA_SCALE * LOG2E)
    q_abs = jnp.pad(jnp.transpose(q_abs, (1, 0, 2)), ((0, 0), (0, 16 - MLA_HEADS), (0, 0))).astype(BF16)
    q_rope = qb_s[..., MLA_NOPE:] * gk[MLA_NOPE:] * (MLA_SCALE * LOG2E)
    q_rope = jnp.pad(q_rope, ((0, 0), (0, 16 - MLA_HEADS), (0, 0))).astype(BF16)
    wukt_flat = wukt.reshape(MLA_HEADS * MLA_NOPE, MLA_KV_LORA).astype(BF16)

    dist_last = PAGE_SIZE - jnp.arange(PAGE_SIZE, dtype=jnp.int32)
    bl = jnp.transpose(tbl[dist_last], (1, 0))
    bl = jnp.repeat(bl.reshape(DA_KV_HEADS, DA_GROUP, 1, PAGE_SIZE), 2, axis=2)
    bl = bl.reshape(DA_KV_HEADS, 2 * DA_GROUP, PAGE_SIZE)
    bias_last = jnp.pad(bl, ((0, 0), (0, 16 - 2 * DA_GROUP), (PAGE_SIZE, 0)))

    s_own_a = jnp.einsum('bngmd,bnmd->bngm', qa_s, ka_s) * (DA_SCALE * LOG2E)
    s_own_a = s_own_a + tbl[0].reshape(1, DA_KV_HEADS, DA_GROUP, 1)
    s_own_a = jnp.pad(s_own_a.reshape(B, DA_KV_HEADS, 2 * DA_GROUP), ((0, 0), (0, 0), (0, 16 - 2 * DA_GROUP)))
    s_own_a = jnp.broadcast_to(s_own_a[..., None], (B, DA_KV_HEADS, 16, 128))
    v_own_a = va_s.reshape(B, DA_KV_HEADS, 1, 2 * DA_D)
    kn_s = _mm(ckv_s.astype(BF16), w_uk_flat, tm=B).reshape(B, MLA_HEADS, MLA_NOPE)
    kb_s = jnp.concatenate(
        [kn_s, jnp.broadcast_to(kpe_s[:, None, :], (B, MLA_HEADS, MLA_ROPE))], axis=-1)
    kb_s = _rms(kb_s, gk)
    s_own_b = jnp.sum(qb_s * kb_s, axis=-1) * (MLA_SCALE * LOG2E)
    s_own_b = jnp.pad(s_own_b, ((0, 0), (0, 16 - MLA_HEADS)))
    s_own_b = jnp.broadcast_to(s_own_b[..., None], (B, 16, 128))
    c_own = ckv_s.reshape(B, 1, MLA_KV_LORA)

    n_phys = cache_da_k.shape[1]
    ra, rb = _decode_attention(
        page_table, qa_mat, q_abs, q_rope, wukt_flat, bias_last,
        s_own_a, v_own_a, s_own_b, c_own,
        jnp.transpose(cache_da_k[layer], (0, 2, 3, 4, 1)).reshape(n_phys, DA_KV_HEADS * 2 * DA_D, PAGE_SIZE),
        cache_da_v[layer].reshape(n_phys, PAGE_SIZE * DA_KV_HEADS, 2 * DA_D),
        cache_mla_ckv[layer],
        jnp.transpose(cache_mla_kpe[layer], (0, 2, 1)))
    ra = ra[:, :, :2 * DA_GROUP].reshape(B, DA_HEADS, 2, 2 * DA_D)
    oa_s = ra[:, :, 0] - lam * ra[:, :, 1]
    rb_h = jnp.transpose(rb[:, :MLA_HEADS], (1, 0, 2)).astype(BF16)
    wuv_h = jnp.transpose(w_uv[layer], (1, 0, 2))
    ob_s = jnp.transpose(_bmm(rb_h, wuv_h, tm=B), (1, 0, 2)).reshape(B, MLA_HEADS * MLA_V)

    x_all = jnp.concatenate([xp, xs], axis=0)
    oa_all = jnp.concatenate([oa_p, oa_s], axis=0)
    oa_all = (_rms(oa_all, g_oa[layer]) * (1.0 - lam_init)).reshape(S + B, DA_HEADS * 2 * DA_D)
    ob_all = jnp.concatenate([ob_p, ob_s], axis=0)
    ya = _mm(oa_all.astype(BF16), w_oa[layer])
    yb = _mm(ob_all.astype(BF16), w_ob[layer])
    g_a = jnp.concatenate([ga_p, ga_s], axis=0)
    g_b = jnp.concatenate([gb_p, gb_s], axis=0)
    mix = (jax.nn.sigmoid(g_a) * ya + jax.nn.sigmoid(g_b) * yb).astype(BF16)
    y_mix = _mm(mix, w_out[layer])

    def per_row(m):
        return jnp.concatenate([jnp.broadcast_to(m[:1], (S, D)), m[1:]], axis=0)

    x1 = x_all + per_row(gt1) * y_mix
    h2 = _rms(x1, g_norm2[layer]) * (1.0 + per_row(sc2)) + per_row(sh2)
    y_moe = _moe(h2, w_router[layer], b_router[layer], w_gu[layer], b_gu[layer],
                 w_down[layer], b_down[layer])
    x2 = x1 + per_row(gt2) * y_moe

    y_prompt = x2[:S][None]
    y_sample = x2[S:][:, None]
    return (y_prompt, y_sample,
            ka_p[None, None], va_p[None, None], ckv_p[None, None], kpe_p[None, None],
            ka_s[None, :, None], va_s[None, :, None], ckv_s[None, :, None], kpe_s[None, :, None])
```

```python
import functools
import math

import jax
import jax.numpy as jnp
from jax import lax
from jax.experimental import pallas as pl
from jax.experimental.pallas import tpu as pltpu

F32 = jnp.float32
BF16 = jnp.bfloat16

PAGE_SIZE = 128
DA_HEADS = 8
DA_KV_HEADS = 2
DA_GROUP = DA_HEADS // DA_KV_HEADS
DA_D = 64
DA_SCALE = 1.0 / math.sqrt(DA_D)
MLA_HEADS = 8
MLA_NOPE = 128
MLA_ROPE = 64
MLA_V = 128
MLA_KV_LORA = 512
MLA_QK_D = MLA_NOPE + MLA_ROPE
MLA_SCALE = 1.0 / math.sqrt(MLA_QK_D)
ROPE_THETA = 10000.0
REL_BUCKETS = 32
REL_MAX_DIST = 128
N_EXPERTS = 32
TOP_K = 4
SWIGLU_LIMIT = 7.0
SWIGLU_ALPHA = 1.702
NORM_EPS = 1e-6

LOG2E = math.log2(math.e)
MASK_VALUE = -1e30
VMEM_LIMIT = 56 * 1024 * 1024

ATTN_TILE = 512
DEC_PAGES = 16
MOE_TM = 512
MOE_TF = 512

_NT = (((1,), (1,)), ((), ()))


def _cparams(sem):
    return pltpu.CompilerParams(dimension_semantics=sem, vmem_limit_bytes=VMEM_LIMIT)


def _split_bf16(x):
    hi = x.astype(BF16)
    lo = (x - hi.astype(F32)).astype(BF16)
    return hi, lo


def _mm_kernel(*refs, nk, split, has_bias):
    if has_bias:
        a_ref, b_ref, bias_ref, o_ref, acc_ref = refs
    else:
        a_ref, b_ref, o_ref, acc_ref = refs
        bias_ref = None
    k = pl.program_id(3)

    @pl.when(k == 0)
    def _():
        acc_ref[...] = jnp.zeros_like(acc_ref)

    if split:
        a_hi, a_lo = _split_bf16(a_ref[...].astype(F32))
        b_hi, b_lo = _split_bf16(b_ref[...].astype(F32))
        part = jnp.dot(a_hi, b_hi, preferred_element_type=F32)
        part += jnp.dot(a_lo, b_hi, preferred_element_type=F32)
        part += jnp.dot(a_hi, b_lo, preferred_element_type=F32)
    else:
        part = jnp.dot(a_ref[...].astype(BF16), b_ref[...].astype(BF16),
                       preferred_element_type=F32)
    acc_ref[...] += part

    @pl.when(k == nk - 1)
    def _():
        r = acc_ref[...]
        if has_bias:
            r = r + bias_ref[...]
        o_ref[...] = r.astype(o_ref.dtype)


def _pick(n, cands):
    for c in cands:
        if n % c == 0:
            return c
    return n


def _bmm(a, b, bias=None, *, out_dtype=F32, split=False, tm=None, tn=None, tk=None):
    G, M, K = a.shape
    _, _, N = b.shape
    tm = tm or _pick(M, (640, 512, 256, 128))
    tk = tk or K
    tn = tn or (N if N <= 1024 else 1024)
    nm, nn, nk = M // tm, pl.cdiv(N, tn), K // tk
    assert M % tm == 0 and K % tk == 0
    in_specs = [
        pl.BlockSpec((None, tm, tk), lambda g, j, i, k: (g, i, k)),
        pl.BlockSpec((None, tk, tn), lambda g, j, i, k: (g, k, j)),
    ]
    args = [a, b]
    if bias is not None:
        in_specs.append(pl.BlockSpec((None, 1, tn), lambda g, j, i, k: (g, 0, j)))
        args.append(bias.reshape(G, 1, N).astype(F32))
    return pl.pallas_call(
        functools.partial(_mm_kernel, nk=nk, split=split, has_bias=bias is not None),
        grid=(G, nn, nm, nk),
        in_specs=in_specs,
        out_specs=pl.BlockSpec((None, tm, tn), lambda g, j, i, k: (g, i, j)),
        out_shape=jax.ShapeDtypeStruct((G, M, N), out_dtype),
        scratch_shapes=[pltpu.VMEM((tm, tn), F32)],
        compiler_params=_cparams(("parallel", "parallel", "parallel", "arbitrary")),
    )(*args)


def _mm(a, b, bias=None, **kw):
    return _bmm(a[None], b[None], None if bias is None else bias[None], **kw)[0]


def _flash_kernel(qi_ref, ki_ref, q_ref, k_ref, v_ref, bias_ref, o_ref,
                  m_ref, l_ref, acc_ref, *, hp, kmap, vmap, bmap, dv):
    t = pl.program_id(1)
    qi = qi_ref[t]
    ki = ki_ref[t]

    @pl.when(ki == 0)
    def _():
        m_ref[...] = jnp.full_like(m_ref, MASK_VALUE)
        l_ref[...] = jnp.zeros_like(l_ref)
        acc_ref[...] = jnp.zeros_like(acc_ref)

    def update(near):
        for c in range(hp):
            s = lax.dot_general(q_ref[c], k_ref[kmap[c]], _NT, preferred_element_type=F32)
            if near:
                s = s + bias_ref[bmap[c], qi - ki]
            m_prev = m_ref[c]
            m_new = jnp.maximum(m_prev, jnp.max(s, axis=1, keepdims=True))
            alpha = jnp.exp2(m_prev - m_new)
            p = jnp.exp2(s - m_new[:, :1])
            l_ref[c] = alpha * l_ref[c] + jnp.sum(p, axis=1, keepdims=True)
            acc_ref[c] = alpha[:, :dv] * acc_ref[c] + jnp.dot(
                p.astype(BF16), v_ref[vmap[c]], preferred_element_type=F32)
            m_ref[c] = m_new

    @pl.when(qi - ki <= 1)
    def _():
        update(True)

    @pl.when(qi - ki > 1)
    def _():
        update(False)

    @pl.when(ki == qi)
    def _():
        for c in range(hp):
            o_ref[:, c * dv:(c + 1) * dv] = (acc_ref[c] / l_ref[c][:, :dv]).astype(o_ref.dtype)


def _flash(q, k, v, bias, *, groups, hp, kb, vb, bb, kmap, vmap, bmap, tile):
    _, S, dk = q.shape
    dv = v.shape[-1]
    T = tile
    assert dv == 128 and S % T == 0
    nq = S // T
    qi_l, ki_l = [], []
    for i in range(nq):
        for j in range(i + 1):
            qi_l.append(i)
            ki_l.append(j)
    qi_arr = jnp.asarray(qi_l, jnp.int32)
    ki_arr = jnp.asarray(ki_l, jnp.int32)
    grid_spec = pltpu.PrefetchScalarGridSpec(
        num_scalar_prefetch=2,
        grid=(groups, len(qi_l)),
        in_specs=[
            pl.BlockSpec((hp, T, dk), lambda g, t, qi, ki: (g, qi[t], 0)),
            pl.BlockSpec((kb, T, dk), lambda g, t, qi, ki: (g, ki[t], 0)),
            pl.BlockSpec((vb, T, dv), lambda g, t, qi, ki: (g, ki[t], 0)),
            pl.BlockSpec((bb, 2, T, T), lambda g, t, qi, ki: (g, 0, 0, 0)),
        ],
        out_specs=pl.BlockSpec((T, hp * dv), lambda g, t, qi, ki: (qi[t], g)),
        scratch_shapes=[pltpu.VMEM((hp, T, 128), F32),
                        pltpu.VMEM((hp, T, 128), F32),
                        pltpu.VMEM((hp, T, dv), F32)],
    )
    return pl.pallas_call(
        functools.partial(_flash_kernel, hp=hp, kmap=kmap, vmap=vmap, bmap=bmap, dv=dv),
        grid_spec=grid_spec,
        out_shape=jax.ShapeDtypeStruct((S, groups * hp * dv), F32),
        compiler_params=_cparams(("parallel", "arbitrary")),
    )(qi_arr, ki_arr, q, k, v, bias)


def _merge_partial(state, part):
    m, l, acc = state
    m_k, l_k, acc_k = part
    m_new = jnp.maximum(m, m_k)
    a = jnp.exp2(m - m_new)
    b = jnp.exp2(m_k - m_new)
    return m_new, a * l + b * l_k, a * acc + b * acc_k


def _local_softmax(s):
    m_k = jnp.max(s, axis=1, keepdims=True)
    p = jnp.exp2(s - m_k)
    return m_k, jnp.sum(p, axis=1, keepdims=True), p.astype(BF16)


def _decode_kernel(pt_ref, qa_ref, qabs_ref, qrope_ref, wukt_ref, biasl_ref,
                   soa_ref, voa_ref, sob_ref, cown_ref, *rest, pg, nch):
    kt = rest[0:pg]
    vv = rest[pg:2 * pg]
    ckv = rest[2 * pg:3 * pg]
    pt = rest[3 * pg:4 * pg]
    oa_ref, ob_ref = rest[4 * pg:4 * pg + 2]
    lhs_ref, cbf, m_a, l_a, acc_a, m_b, l_b, acc_b = rest[4 * pg + 2:]
    b = pl.program_id(0)
    c = pl.program_id(1)
    nw = MLA_HEADS * MLA_NOPE
    ps = PAGE_SIZE
    kvw = 2 * DA_D

    @pl.when((b == 0) & (c == 0))
    def _():
        lhs_ref[0:nw, :] = wukt_ref[...]

    @pl.when(c == 0)
    def _():
        lhs_ref[nw:nw + 16, :] = qabs_ref[0]
        m_a[...] = jnp.full_like(m_a, MASK_VALUE)
        l_a[...] = jnp.zeros_like(l_a)
        acc_a[...] = jnp.zeros_like(acc_a)
        m_b[...] = jnp.full_like(m_b, MASK_VALUE)
        l_b[...] = jnp.zeros_like(l_b)
        acc_b[...] = jnp.zeros_like(acc_b)

    is_last = (c == nch - 1).astype(F32)
    st_b = (m_b[:, 0:1], l_b[:, 0:1], acc_b[...])
    st_a = [(m_a[n][:, 0:1], l_a[n][:, 0:1], acc_a[n]) for n in range(DA_KV_HEADS)]

    n_tok = pg * ps
    for j in range(pg):
        cbf[j * ps:(j + 1) * ps, :] = ckv[j][...].astype(BF16)
    pe = jnp.concatenate([pt[j][...] for j in range(pg)], axis=1)
    s_rope = jnp.dot(qrope_ref[0], pe.astype(BF16), preferred_element_type=F32)
    ssq_pe = jnp.sum(pe * pe, axis=0, keepdims=True)

    part_a = []
    for n in range(DA_KV_HEADS):
        rows = slice(n * kvw, (n + 1) * kvw)
        ktn = jnp.concatenate([kt[j][rows, :] for j in range(pg)], axis=1).astype(BF16)
        s = jnp.dot(qa_ref[0, n], ktn, preferred_element_type=F32)
        s = jnp.concatenate([s[:, :n_tok - 2 * ps],
                             s[:, n_tok - 2 * ps:] + biasl_ref[n] * is_last], axis=1)
        part_a.append(_local_softmax(s))

    big = lax.dot_general(lhs_ref[...], cbf[...], _NT, preferred_element_type=F32)
    kn = big[0:nw]
    ssq_kn = jnp.sum((kn * kn).reshape(MLA_HEADS, MLA_NOPE, n_tok), axis=1)

    for n in range(DA_KV_HEADS):
        m_k, l_k, p = part_a[n]
        vn = jnp.concatenate([vv[j][pl.ds(n, ps, stride=DA_KV_HEADS), :] for j in range(pg)],
                             axis=0).astype(BF16)
        st_a[n] = _merge_partial(st_a[n], (m_k, l_k, jnp.dot(p, vn, preferred_element_type=F32)))

    rinv = lax.rsqrt((ssq_kn + ssq_pe) * (1.0 / MLA_QK_D) + NORM_EPS)
    s_b = (big[nw:nw + MLA_HEADS] + s_rope[0:MLA_HEADS]) * rinv
    s_b = jnp.concatenate([s_b, jnp.zeros_like(s_b)], axis=0)
    m_k, l_k, p = _local_softmax(s_b)
    st_b = _merge_partial(st_b, (m_k, l_k, jnp.dot(p, cbf[...], preferred_element_type=F32)))

    m_b[...] = jnp.broadcast_to(st_b[0], m_b.shape)
    l_b[...] = jnp.broadcast_to(st_b[1], l_b.shape)
    acc_b[...] = st_b[2]
    for n in range(DA_KV_HEADS):
        m_a[n] = jnp.broadcast_to(st_a[n][0], (16, 128))
        l_a[n] = jnp.broadcast_to(st_a[n][1], (16, 128))
        acc_a[n] = st_a[n][2]

    @pl.when(c == nch - 1)
    def _():
        for n in range(DA_KV_HEADS):
            m_f, l_f, acc_f = _merge_partial(
                st_a[n], (soa_ref[0, n][:, 0:1], jnp.ones((16, 1), F32), voa_ref[0, n]))
            oa_ref[0, n] = acc_f / l_f
        m_f, l_f, acc_f = _merge_partial(
            st_b, (sob_ref[0][:, 0:1], jnp.ones((16, 1), F32), cown_ref[0]))
        ob_ref[0] = acc_f / l_f


def _decode_attention(page_table, qa_mat, q_abs, q_rope, wukt, bias_last,
                      s_own_a, v_own_a, s_own_b, c_own,
                      cache_k, cache_v, cache_c, cache_p):
    B, n_pages = page_table.shape
    pg = DEC_PAGES
    assert n_pages % pg == 0 and pg % 2 == 0
    nch = n_pages // pg
    nw = MLA_HEADS * MLA_NOPE

    def seq_spec(shape):
        nd = len(shape)
        return pl.BlockSpec((1,) + shape, lambda b, c, pt: (b,) + (0,) * nd)

    def const_spec(shape):
        nd = len(shape)
        return pl.BlockSpec(shape, lambda b, c, pt: (0,) * nd)

    def page_spec(arr, j):
        return pl.BlockSpec((None,) + arr.shape[1:],
                            lambda b, c, pt: (pt[b * n_pages + c * pg + j], 0, 0))

    in_specs = [
        seq_spec((DA_KV_HEADS, 16, 2 * DA_D)),
        seq_spec((16, MLA_KV_LORA)),
        seq_spec((16, MLA_ROPE)),
        const_spec((nw, MLA_KV_LORA)),
        const_spec((DA_KV_HEADS, 16, 2 * PAGE_SIZE)),
        seq_spec((DA_KV_HEADS, 16, 128)),
        seq_spec((DA_KV_HEADS, 1, 128)),
        seq_spec((16, 128)),
        seq_spec((1, MLA_KV_LORA)),
    ]
    args = [qa_mat, q_abs, q_rope, wukt, bias_last, s_own_a, v_own_a, s_own_b, c_own]
    for arr in (cache_k, cache_v, cache_c, cache_p):
        for j in range(pg):
            in_specs.append(page_spec(arr, j))
            args.append(arr)
    grid_spec = pltpu.PrefetchScalarGridSpec(
        num_scalar_prefetch=1,
        grid=(B, nch),
        in_specs=in_specs,
        out_specs=[seq_spec((DA_KV_HEADS, 16, 128)), seq_spec((16, MLA_KV_LORA))],
        scratch_shapes=[
            pltpu.VMEM((nw + 16, MLA_KV_LORA), BF16),
            pltpu.VMEM((pg * PAGE_SIZE, MLA_KV_LORA), BF16),
            pltpu.VMEM((DA_KV_HEADS, 16, 128), F32),
            pltpu.VMEM((DA_KV_HEADS, 16, 128), F32),
            pltpu.VMEM((DA_KV_HEADS, 16, 128), F32),
            pltpu.VMEM((16, 128), F32),
            pltpu.VMEM((16, 128), F32),
            pltpu.VMEM((16, MLA_KV_LORA), F32),
        ],
    )
    return pl.pallas_call(
        functools.partial(_decode_kernel, pg=pg, nch=nch),
        grid_spec=grid_spec,
        out_shape=[jax.ShapeDtypeStruct((B, DA_KV_HEADS, 16, 128), F32),
                   jax.ShapeDtypeStruct((B, 16, MLA_KV_LORA), F32)],
        compiler_params=_cparams(("arbitrary", "arbitrary")),
    )(page_table.reshape(-1), *args)


def _moe_kernel(te_ref, tv_ref, x_ref, wg_ref, wu_ref, bg_ref, bu_ref, wd_ref, bd_ref,
                rw_ref, o_ref, acc_ref, *, nf):
    i = pl.program_id(0)
    j = pl.program_id(1)
    valid = tv_ref[i] == 1

    @pl.when(valid)
    def _():
        @pl.when(j == 0)
        def _():
            acc_ref[...] = jnp.zeros_like(acc_ref)

        x = x_ref[...]
        g = jnp.dot(x, wg_ref[...].astype(BF16), preferred_element_type=F32) + bg_ref[...]
        u = jnp.dot(x, wu_ref[...].astype(BF16), preferred_element_type=F32) + bu_ref[...]
        g = jnp.minimum(g, SWIGLU_LIMIT)
        u = jnp.clip(u, -SWIGLU_LIMIT, SWIGLU_LIMIT)
        act = (u + 1.0) * (g * jax.nn.sigmoid(SWIGLU_ALPHA * g))
        acc_ref[...] += jnp.dot(act.astype(BF16), wd_ref[...].astype(BF16),
                                preferred_element_type=F32)

        @pl.when(j == nf - 1)
        def _():
            o_ref[...] = (acc_ref[...] + bd_ref[...]) * rw_ref[...]

    @pl.when(jnp.logical_not(valid) & (j == nf - 1))
    def _():
        o_ref[...] = jnp.zeros_like(o_ref)


def _moe_ffn(tile_expert, tile_valid, x_sorted, row_w, w_gu, b_gu, w_down, b_down):
    R, D = x_sorted.shape
    E, _, F2 = w_gu.shape
    dff = F2 // 2
    tm, tf = MOE_TM, MOE_TF
    nf = dff // tf
    n_tiles = R // tm
    grid_spec = pltpu.PrefetchScalarGridSpec(
        num_scalar_prefetch=2,
        grid=(n_tiles, nf),
        in_specs=[
            pl.BlockSpec((tm, D), lambda i, j, te, tv: (i, 0)),
            pl.BlockSpec((None, D, tf), lambda i, j, te, tv: (te[i], 0, j)),
            pl.BlockSpec((None, D, tf), lambda i, j, te, tv: (te[i], 0, nf + j)),
            pl.BlockSpec((None, 1, tf), lambda i, j, te, tv: (te[i], 0, j)),
            pl.BlockSpec((None, 1, tf), lambda i, j, te, tv: (te[i], 0, nf + j)),
            pl.BlockSpec((None, tf, D), lambda i, j, te, tv: (te[i], j, 0)),
            pl.BlockSpec((None, 1, D), lambda i, j, te, tv: (te[i], 0, 0)),
            pl.BlockSpec((tm, 1), lambda i, j, te, tv: (i, 0)),
        ],
        out_specs=pl.BlockSpec((tm, D), lambda i, j, te, tv: (i, 0)),
        scratch_shapes=[pltpu.VMEM((tm, D), F32)],
    )
    return pl.pallas_call(
        functools.partial(_moe_kernel, nf=nf),
        grid_spec=grid_spec,
        out_shape=jax.ShapeDtypeStruct((R, D), F32),
        compiler_params=_cparams(("arbitrary", "arbitrary")),
    )(tile_expert, tile_valid, x_sorted, w_gu, w_gu,
      b_gu.reshape(E, 1, F2), b_gu.reshape(E, 1, F2), w_down, b_down.reshape(E, 1, D), row_w)


def _moe(h_parts, w_router, b_router, w_gu, b_gu, w_down, b_down):
    E = w_router.shape[-1]
    tm = MOE_TM
    logits = jnp.concatenate([_mm(h, w_router, b_router, split=True) for h in h_parts], axis=0)
    h_bf = jnp.concatenate([h.astype(BF16) for h in h_parts], axis=0)
    N = h_bf.shape[0]
    top_val, top_idx = lax.top_k(logits, TOP_K)
    top_w = jax.nn.softmax(top_val, axis=-1)

    P = N * TOP_K
    n_tiles = pl.cdiv(P, tm) + E
    e_flat = top_idx.reshape(P).astype(jnp.int32)
    order = jnp.argsort(e_flat, stable=True).astype(jnp.int32)
    rank = jnp.argsort(order).astype(jnp.int32)
    cnt = jnp.sum(jax.nn.one_hot(e_flat, E, dtype=jnp.int32), axis=0)
    padded = ((cnt + tm - 1) // tm) * tm
    pad_end = jnp.cumsum(padded)
    pad_off = pad_end - padded
    sort_off = jnp.cumsum(cnt) - cnt
    pos = (rank + (pad_off - sort_off)[e_flat]).reshape(N, TOP_K)

    tile_start = jnp.arange(n_tiles, dtype=jnp.int32) * tm
    tile_valid = (tile_start < pad_end[-1]).astype(jnp.int32)
    tile_expert = jnp.minimum(jnp.searchsorted(pad_end, tile_start, side='right'), E - 1)
    last_valid = jnp.maximum(pad_end[-1] // tm - 1, 0)
    tile_expert = jnp.where(tile_valid == 1, tile_expert, tile_expert[last_valid]).astype(jnp.int32)

    in_expert = tile_start[:, None] + jnp.arange(tm, dtype=jnp.int32)[None, :] - pad_off[tile_expert][:, None]
    row_valid = (in_expert < cnt[tile_expert][:, None]) & (tile_valid[:, None] == 1)
    sorted_idx = jnp.clip(sort_off[tile_expert][:, None] + in_expert, 0, P - 1)
    pair = order[sorted_idx.reshape(-1)]
    row_valid = row_valid.reshape(-1)
    src_tok = jnp.where(row_valid, pair // TOP_K, 0)
    row_w = jnp.where(row_valid, top_w.reshape(P)[pair], 0.0)

    y = _moe_ffn(tile_expert, tile_valid, h_bf[src_tok], row_w.reshape(-1, 1),
                 w_gu, b_gu, w_down, b_down)
    return jnp.sum(y[pos], axis=1)


def _rms(x, g):
    return x * lax.rsqrt(jnp.mean(x * x, axis=-1, keepdims=True) + NORM_EPS) * g


def _rope(x, pos):
    half = MLA_ROPE // 2
    inv = ROPE_THETA ** (-jnp.arange(half, dtype=F32) / half)
    ang = pos.astype(F32)[:, None] * inv[None, :]
    cos, sin = jnp.cos(ang), jnp.sin(ang)
    if x.ndim == 3:
        cos, sin = cos[:, None, :], sin[:, None, :]
    x1, x2 = x[..., :half], x[..., half:]
    return jnp.concatenate([x1 * cos - x2 * sin, x2 * cos + x1 * sin], axis=-1)


def _rel_bias_by_distance(rel_table, max_n):
    n = jnp.arange(max_n + 1, dtype=jnp.int32)
    max_exact = REL_BUCKETS // 2
    nf = jnp.maximum(n, 1).astype(F32)
    large = max_exact + (jnp.log(nf / max_exact) / math.log(REL_MAX_DIST / max_exact)
                         * (REL_BUCKETS - max_exact)).astype(jnp.int32)
    large = jnp.minimum(large, REL_BUCKETS - 1)
    bucket = jnp.where(n < max_exact, n, large)
    return rel_table[bucket].astype(F32)


def _toeplitz(v, t):
    h = v.shape[0]
    u = v[:, ::-1]
    flat = jnp.tile(u, (1, t))[:, :t * (2 * t - 1)]
    return flat.reshape(h, t, 2 * t - 1)[:, :, t - 1:]


def _split_z(z):
    da_q = DA_HEADS * 2 * DA_D
    da_kv = DA_KV_HEADS * 2 * DA_D
    mla_q = MLA_HEADS * MLA_QK_D
    d_model = (z.shape[-1] - da_q - 2 * da_kv - mla_q - MLA_KV_LORA - MLA_ROPE) // 2
    sizes = (da_q, da_kv, da_kv, mla_q, MLA_KV_LORA, MLA_ROPE, d_model, d_model)
    out, o = [], 0
    for s in sizes:
        out.append(z[:, o:o + s])
        o += s
    return out


def _mixer_inputs(z, pos, g_qa, g_ka, g_qb, g_ckv):
    za_q, za_k, za_v, zb_q, zb_ckv, zb_kpe, g_a, g_b = _split_z(z)
    T = z.shape[0]
    qa = _rms(za_q.reshape(T, DA_KV_HEADS, DA_GROUP, 2, DA_D), g_qa)
    ka = _rms(za_k.reshape(T, DA_KV_HEADS, 2, DA_D), g_ka)
    va = za_v.reshape(T, DA_KV_HEADS, 2 * DA_D)
    qb = zb_q.reshape(T, MLA_HEADS, MLA_QK_D)
    qb = jnp.concatenate([qb[..., :MLA_NOPE], _rope(qb[..., MLA_NOPE:], pos)], axis=-1)
    qb = _rms(qb, g_qb)
    ckv = _rms(zb_ckv, g_ckv)
    kpe = _rope(zb_kpe, pos)
    return qa, ka, va, qb, ckv, kpe, g_a, g_b


def kernel(x_prompt, x_sample, cache_da_k, cache_da_v, cache_mla_ckv, cache_mla_kpe, page_table, c_prompt, c_sample, rel_table, w_ada, b_ada, g_norm1, w_in, g_qa, g_ka, w_lambda, g_oa, g_qb, g_ckv, w_uk, g_kb, w_uv, w_oa, w_ob, w_out, g_norm2, w_router, b_router, w_gu, b_gu, w_down, b_down):
    depth = w_in.shape[0]
    assert depth == 1 and x_prompt.shape[0] == 1 and x_sample.shape[1] == 1
    assert PAGE_SIZE >= REL_MAX_DIST
    layer = 0
    S, D = x_prompt.shape[1], x_prompt.shape[2]
    B = x_sample.shape[0]
    n_pages = page_table.shape[1]
    past_len = n_pages * PAGE_SIZE
    T = ATTN_TILE

    lam_init = 0.8 - 0.6 * math.exp(-0.3 * layer)
    lw = w_lambda[layer]
    lam = jnp.exp(jnp.sum(lw[0] * lw[1])) - jnp.exp(jnp.sum(lw[2] * lw[3])) + lam_init

    xp = x_prompt[0]
    xs = x_sample[:, 0]

    c_all = jnp.concatenate([c_prompt, c_sample], axis=0)
    n_c = c_all.shape[0]
    n_c_pad = -(-n_c // 16) * 16
    c_act = jnp.pad(jax.nn.silu(c_all), ((0, n_c_pad - n_c), (0, 0)))
    mods = _mm(c_act, w_ada[layer], b_ada[layer], tm=n_c_pad, tn=512)[:n_c]
    sh1, sc1, gt1, sh2, sc2, gt2 = [mods[:, i * D:(i + 1) * D] for i in range(6)]

    def modulate(x, g, sc, sh):
        return _rms(x, g) * (1.0 + sc) + sh

    h_p = modulate(xp, g_norm1[layer], sc1[:1], sh1[:1])
    h_s = modulate(xs, g_norm1[layer], sc1[1:], sh1[1:])
    z_p = _mm(h_p.astype(BF16), w_in[layer])
    z_s = _mm(h_s.astype(BF16), w_in[layer])

    pos_p = jnp.arange(S, dtype=jnp.int32)
    pos_s = jnp.full((B,), past_len, jnp.int32)
    qa_p, ka_p, va_p, qb_p, ckv_p, kpe_p, ga_p, gb_p = _mixer_inputs(
        z_p, pos_p, g_qa[layer], g_ka[layer], g_qb[layer], g_ckv[layer])
    qa_s, ka_s, va_s, qb_s, ckv_s, kpe_s, ga_s, gb_s = _mixer_inputs(
        z_s, pos_s, g_qa[layer], g_ka[layer], g_qb[layer], g_ckv[layer])

    tbl = _rel_bias_by_distance(rel_table, REL_MAX_DIST)
    tbl = (tbl - tbl[REL_MAX_DIST][None, :]) * LOG2E

    w_uk_flat = w_uk[layer].reshape(MLA_KV_LORA, MLA_HEADS * MLA_NOPE)
    w_uv_flat = w_uv[layer].reshape(MLA_KV_LORA, MLA_HEADS * MLA_V)
    causal = (jnp.arange(T)[:, None] >= jnp.arange(T)[None, :])[None]
    by_dist = jnp.pad(jnp.transpose(tbl), ((0, 0), (T, T - REL_MAX_DIST - 1)))
    bias_diag = jnp.where(causal, _toeplitz(by_dist, T), MASK_VALUE)
    bias_off = _toeplitz(jnp.pad(by_dist[:, T:], ((0, 0), (0, T))), T)
    bias_da = jnp.stack([bias_diag, bias_off], axis=1)
    mask_b = jnp.stack([jnp.where(causal, 0.0, MASK_VALUE),
                        jnp.zeros((1, T, T), F32)], axis=1)

    eye2 = jnp.eye(2, dtype=F32)

    def da_q_padded(qa):
        q = jnp.transpose(qa, (1, 2, 3, 0, 4)) * (DA_SCALE * LOG2E)
        q = q[:, :, :, :, None, :] * eye2[None, None, :, None, :, None]
        return q.reshape(DA_HEADS * 2, qa.shape[0], 2 * DA_D)

    q_da = da_q_padded(qa_p).astype(BF16)
    k_da = jnp.transpose(ka_p.reshape(S, DA_KV_HEADS, 2 * DA_D), (1, 0, 2)).astype(BF16)
    v_da = jnp.transpose(va_p, (1, 0, 2)).astype(BF16)
    o_da = _flash(q_da, k_da, v_da, bias_da, groups=DA_KV_HEADS, hp=2 * DA_GROUP, kb=1, vb=1,
                  bb=DA_GROUP, kmap=[0] * (2 * DA_GROUP), vmap=[0] * (2 * DA_GROUP),
                  bmap=[c // 2 for c in range(2 * DA_GROUP)], tile=T)
    o_da = o_da.reshape(S, DA_HEADS, 2, 2 * DA_D)
    oa_p = o_da[:, :, 0] - lam * o_da[:, :, 1]

    kn_p = _mm(ckv_p.astype(BF16), w_uk_flat).reshape(S, MLA_HEADS, MLA_NOPE)
    kb_p = jnp.concatenate(
        [kn_p, jnp.broadcast_to(kpe_p[:, None, :], (S, MLA_HEADS, MLA_ROPE))], axis=-1)
    kb_p = _rms(kb_p, g_kb[layer])
    vb_p = _mm(ckv_p.astype(BF16), w_uv_flat).reshape(S, MLA_HEADS, MLA_V)
    q_ml = jnp.transpose(qb_p * (MLA_SCALE * LOG2E), (1, 0, 2)).astype(BF16)
    k_ml = jnp.transpose(kb_p, (1, 0, 2)).astype(BF16)
    v_ml = jnp.transpose(vb_p, (1, 0, 2)).astype(BF16)
    hm = list(range(MLA_HEADS))
    ob_p = _flash(q_ml, k_ml, v_ml, mask_b, groups=1, hp=MLA_HEADS, kb=MLA_HEADS, vb=MLA_HEADS,
                  bb=1, kmap=hm, vmap=hm, bmap=[0] * MLA_HEADS, tile=T)

    qa_mat = jnp.transpose(da_q_padded(qa_s).reshape(DA_KV_HEADS, 2 * DA_GROUP, B, 2 * DA_D),
                           (2, 0, 1, 3))
    qa_mat = jnp.pad(qa_mat, ((0, 0), (0, 0), (0, 16 - 2 * DA_GROUP), (0, 0))).astype(BF16)
    gk = g_kb[layer]
    qn = jnp.transpose(qb_s[..., :MLA_NOPE] * gk[:MLA_NOPE], (1, 0, 2))
    wukt = jnp.transpose(w_uk[layer], (1, 2, 0))
    q_abs = _bmm(qn, wukt, split=True) * (MLA_SCALE * LOG2E)
    q_abs = jnp.pad(jnp.transpose(q_abs, (1, 0, 2)), ((0, 0), (0, 16 - MLA_HEADS), (0, 0))).astype(BF16)
    q_rope = qb_s[..., MLA_NOPE:] * gk[MLA_NOPE:] * (MLA_SCALE * LOG2E)
    q_rope = jnp.pad(q_rope, ((0, 0), (0, 16 - MLA_HEADS), (0, 0))).astype(BF16)
    wukt_flat = wukt.reshape(MLA_HEADS * MLA_NOPE, MLA_KV_LORA).astype(BF16)

    dist_last = PAGE_SIZE - jnp.arange(PAGE_SIZE, dtype=jnp.int32)
    bl = jnp.transpose(tbl[dist_last], (1, 0))
    bl = jnp.repeat(bl.reshape(DA_KV_HEADS, DA_GROUP, 1, PAGE_SIZE), 2, axis=2)
    bl = bl.reshape(DA_KV_HEADS, 2 * DA_GROUP, PAGE_SIZE)
    bias_last = jnp.pad(bl, ((0, 0), (0, 16 - 2 * DA_GROUP), (PAGE_SIZE, 0)))

    s_own_a = jnp.einsum('bngmd,bnmd->bngm', qa_s, ka_s) * (DA_SCALE * LOG2E)
    s_own_a = s_own_a + tbl[0].reshape(1, DA_KV_HEADS, DA_GROUP, 1)
    s_own_a = jnp.pad(s_own_a.reshape(B, DA_KV_HEADS, 2 * DA_GROUP), ((0, 0), (0, 0), (0, 16 - 2 * DA_GROUP)))
    s_own_a = jnp.broadcast_to(s_own_a[..., None], (B, DA_KV_HEADS, 16, 128))
    v_own_a = va_s.reshape(B, DA_KV_HEADS, 1, 2 * DA_D)
    kn_s = _mm(ckv_s.astype(BF16), w_uk_flat, tm=B).reshape(B, MLA_HEADS, MLA_NOPE)
    kb_s = jnp.concatenate(
        [kn_s, jnp.broadcast_to(kpe_s[:, None, :], (B, MLA_HEADS, MLA_ROPE))], axis=-1)
    kb_s = _rms(kb_s, gk)
    s_own_b = jnp.sum(qb_s * kb_s, axis=-1) * (MLA_SCALE * LOG2E)
    s_own_b = jnp.pad(s_own_b, ((0, 0), (0, 16 - MLA_HEADS)))
    s_own_b = jnp.broadcast_to(s_own_b[..., None], (B, 16, 128))
    c_own = ckv_s.reshape(B, 1, MLA_KV_LORA)

    n_phys = cache_da_k.shape[1]
    ra, rb = _decode_attention(
        page_table, qa_mat, q_abs, q_rope, wukt_flat, bias_last,
        s_own_a, v_own_a, s_own_b, c_own,
        jnp.transpose(cache_da_k[layer], (0, 2, 3, 4, 1)).reshape(n_phys, DA_KV_HEADS * 2 * DA_D, PAGE_SIZE),
        cache_da_v[layer].reshape(n_phys, PAGE_SIZE * DA_KV_HEADS, 2 * DA_D),
        cache_mla_ckv[layer],
        jnp.transpose(cache_mla_kpe[layer], (0, 2, 1)))
    ra = ra[:, :, :2 * DA_GROUP].reshape(B, DA_HEADS, 2, 2 * DA_D)
    oa_s = ra[:, :, 0] - lam * ra[:, :, 1]
    rb_h = jnp.transpose(rb[:, :MLA_HEADS], (1, 0, 2)).astype(BF16)
    wuv_h = jnp.transpose(w_uv[layer], (1, 0, 2))
    ob_s = jnp.transpose(_bmm(rb_h, wuv_h, tm=B), (1, 0, 2)).reshape(B, MLA_HEADS * MLA_V)

    def post_attention(x, oa, ob, g_a, g_b, rows):
        oa = (_rms(oa, g_oa[layer]) * (1.0 - lam_init)).reshape(x.shape[0], DA_HEADS * 2 * DA_D)
        ya = _mm(oa.astype(BF16), w_oa[layer])
        yb = _mm(ob.astype(BF16), w_ob[layer])
        mix = (jax.nn.sigmoid(g_a) * ya + jax.nn.sigmoid(g_b) * yb).astype(BF16)
        x1 = x + gt1[rows] * _mm(mix, w_out[layer])
        h2 = _rms(x1, g_norm2[layer]) * (1.0 + sc2[rows]) + sh2[rows]
        return x1, h2

    p_rows, s_rows = slice(0, 1), slice(1, None)
    x1_p, h2_p = post_attention(xp, oa_p, ob_p, ga_p, gb_p, p_rows)
    x1_s, h2_s = post_attention(xs, oa_s, ob_s, ga_s, gb_s, s_rows)
    y_moe = _moe([h2_p, h2_s], w_router[layer], b_router[layer], w_gu[layer], b_gu[layer],
                 w_down[layer], b_down[layer])
    y_prompt = (x1_p + gt2[p_rows] * y_moe[:S])[None]
    y_sample = (x1_s + gt2[s_rows] * y_moe[S:])[:, None]
    return (y_prompt, y_sample,
            ka_p[None, None], va_p[None, None], ckv_p[None, None], kpe_p[None, None],
            ka_s[None, :, None], va_s[None, :, None], ckv_s[None, :, None], kpe_s[None, :, None])
```

```python
import functools
import math

import jax
import jax.numpy as jnp
from jax import lax
from jax.experimental import pallas as pl
from jax.experimental.pallas import tpu as pltpu

F32 = jnp.float32
BF16 = jnp.bfloat16

PAGE_SIZE = 128
DA_HEADS = 8
DA_KV_HEADS = 2
DA_GROUP = DA_HEADS // DA_KV_HEADS
DA_D = 64
DA_SCALE = 1.0 / math.sqrt(DA_D)
MLA_HEADS = 8
MLA_NOPE = 128
MLA_ROPE = 64
MLA_V = 128
MLA_KV_LORA = 512
MLA_QK_D = MLA_NOPE + MLA_ROPE
MLA_SCALE = 1.0 / math.sqrt(MLA_QK_D)
ROPE_THETA = 10000.0
REL_BUCKETS = 32
REL_MAX_DIST = 128
N_EXPERTS = 32
TOP_K = 4
SWIGLU_LIMIT = 7.0
SWIGLU_ALPHA = 1.702
NORM_EPS = 1e-6

LOG2E = math.log2(math.e)
MASK_VALUE = -1e30
VMEM_LIMIT = 56 * 1024 * 1024

ATTN_TILE = 512
DEC_PAGES = 16
MOE_TM = 576
MOE_TF = 512

_NT = (((1,), (1,)), ((), ()))


def _cparams(sem):
    return pltpu.CompilerParams(dimension_semantics=sem, vmem_limit_bytes=VMEM_LIMIT)


def _split_bf16(x):
    hi = x.astype(BF16)
    lo = (x - hi.astype(F32)).astype(BF16)
    return hi, lo


def _mm_kernel(*refs, nk, split, has_bias):
    if has_bias:
        a_ref, b_ref, bias_ref, o_ref, acc_ref = refs
    else:
        a_ref, b_ref, o_ref, acc_ref = refs
        bias_ref = None
    k = pl.program_id(3)

    @pl.when(k == 0)
    def _():
        acc_ref[...] = jnp.zeros_like(acc_ref)

    if split:
        a_hi, a_lo = _split_bf16(a_ref[...].astype(F32))
        b_hi, b_lo = _split_bf16(b_ref[...].astype(F32))
        part = jnp.dot(a_hi, b_hi, preferred_element_type=F32)
        part += jnp.dot(a_lo, b_hi, preferred_element_type=F32)
        part += jnp.dot(a_hi, b_lo, preferred_element_type=F32)
    else:
        part = jnp.dot(a_ref[...].astype(BF16), b_ref[...].astype(BF16),
                       preferred_element_type=F32)
    acc_ref[...] += part

    @pl.when(k == nk - 1)
    def _():
        r = acc_ref[...]
        if has_bias:
            r = r + bias_ref[...]
        o_ref[...] = r.astype(o_ref.dtype)


def _pick(n, cands):
    for c in cands:
        if n % c == 0:
            return c
    return n


def _bmm(a, b, bias=None, *, out_dtype=F32, split=False, tm=None, tn=None, tk=None):
    G, M, K = a.shape
    _, _, N = b.shape
    tm = tm or _pick(M, (640, 512, 256, 128))
    tk = tk or K
    tn = tn or (N if N <= 1024 else 1024)
    nm, nn, nk = M // tm, pl.cdiv(N, tn), K // tk
    assert M % tm == 0 and K % tk == 0
    in_specs = [
        pl.BlockSpec((None, tm, tk), lambda g, j, i, k: (g, i, k)),
        pl.BlockSpec((None, tk, tn), lambda g, j, i, k: (g, k, j)),
    ]
    args = [a, b]
    if bias is not None:
        in_specs.append(pl.BlockSpec((None, 1, tn), lambda g, j, i, k: (g, 0, j)))
        args.append(bias.reshape(G, 1, N).astype(F32))
    return pl.pallas_call(
        functools.partial(_mm_kernel, nk=nk, split=split, has_bias=bias is not None),
        grid=(G, nn, nm, nk),
        in_specs=in_specs,
        out_specs=pl.BlockSpec((None, tm, tn), lambda g, j, i, k: (g, i, j)),
        out_shape=jax.ShapeDtypeStruct((G, M, N), out_dtype),
        scratch_shapes=[pltpu.VMEM((tm, tn), F32)],
        compiler_params=_cparams(("parallel", "parallel", "parallel", "arbitrary")),
    )(*args)


def _mm(a, b, bias=None, **kw):
    return _bmm(a[None], b[None], None if bias is None else bias[None], **kw)[0]


def _flash_kernel(qi_ref, ki_ref, q_ref, k_ref, v_ref, bias_ref, o_ref,
                  m_ref, l_ref, acc_ref, *, hp, kmap, vmap, bmap, dv):
    t = pl.program_id(1)
    qi = qi_ref[t]
    ki = ki_ref[t]

    @pl.when(ki == 0)
    def _():
        m_ref[...] = jnp.full_like(m_ref, MASK_VALUE)
        l_ref[...] = jnp.zeros_like(l_ref)
        acc_ref[...] = jnp.zeros_like(acc_ref)

    def update(near):
        for c in range(hp):
            s = lax.dot_general(q_ref[c], k_ref[kmap[c]], _NT, preferred_element_type=F32)
            if near:
                s = s + bias_ref[bmap[c], qi - ki]
            m_prev = m_ref[c]
            m_new = jnp.maximum(m_prev, jnp.max(s, axis=1, keepdims=True))
            alpha = jnp.exp2(m_prev - m_new)
            p = jnp.exp2(s - m_new[:, :1])
            psum = p[:, 0:128]
            for w in range(1, p.shape[1] // 128):
                psum = psum + p[:, w * 128:(w + 1) * 128]
            l_ref[c] = alpha * l_ref[c] + psum
            acc_ref[c] = alpha[:, :dv] * acc_ref[c] + jnp.dot(
                p.astype(BF16), v_ref[vmap[c]], preferred_element_type=F32)
            m_ref[c] = m_new

    @pl.when(qi - ki <= 1)
    def _():
        update(True)

    @pl.when(qi - ki > 1)
    def _():
        update(False)

    @pl.when(ki == qi)
    def _():
        for c in range(hp):
            l_row = jnp.sum(l_ref[c], axis=1, keepdims=True)
            o_ref[:, c * dv:(c + 1) * dv] = (acc_ref[c] / l_row).astype(o_ref.dtype)


def _flash(q, k, v, bias, *, groups, hp, kb, vb, bb, kmap, vmap, bmap, tile):
    _, S, dk = q.shape
    dv = v.shape[-1]
    T = tile
    assert dv == 128 and S % T == 0
    nq = S // T
    qi_l, ki_l = [], []
    for i in range(nq):
        for j in range(i + 1):
            qi_l.append(i)
            ki_l.append(j)
    qi_arr = jnp.asarray(qi_l, jnp.int32)
    ki_arr = jnp.asarray(ki_l, jnp.int32)
    grid_spec = pltpu.PrefetchScalarGridSpec(
        num_scalar_prefetch=2,
        grid=(groups, len(qi_l)),
        in_specs=[
            pl.BlockSpec((hp, T, dk), lambda g, t, qi, ki: (g, qi[t], 0)),
            pl.BlockSpec((kb, T, dk), lambda g, t, qi, ki: (g, ki[t], 0)),
            pl.BlockSpec((vb, T, dv), lambda g, t, qi, ki: (g, ki[t], 0)),
            pl.BlockSpec((bb, 2, T, T), lambda g, t, qi, ki: (g, 0, 0, 0)),
        ],
        out_specs=pl.BlockSpec((T, hp * dv), lambda g, t, qi, ki: (qi[t], g)),
        scratch_shapes=[pltpu.VMEM((hp, T, 128), F32),
                        pltpu.VMEM((hp, T, 128), F32),
                        pltpu.VMEM((hp, T, dv), F32)],
    )
    return pl.pallas_call(
        functools.partial(_flash_kernel, hp=hp, kmap=kmap, vmap=vmap, bmap=bmap, dv=dv),
        grid_spec=grid_spec,
        out_shape=jax.ShapeDtypeStruct((S, groups * hp * dv), F32),
        compiler_params=_cparams(("parallel", "arbitrary")),
    )(qi_arr, ki_arr, q, k, v, bias)


def _merge_partial(state, part):
    m, l, acc = state
    m_k, l_k, acc_k = part
    m_new = jnp.maximum(m, m_k)
    a = jnp.exp2(m - m_new)
    b = jnp.exp2(m_k - m_new)
    return m_new, a * l + b * l_k, a * acc + b * acc_k


def _local_softmax(s):
    m_k = jnp.max(s, axis=1, keepdims=True)
    p = jnp.exp2(s - m_k)
    return m_k, jnp.sum(p, axis=1, keepdims=True), p.astype(BF16)


def _decode_kernel(pt_ref, qa_ref, qabs_ref, qrope_ref, wukt_ref, biasl_ref,
                   soa_ref, voa_ref, sob_ref, cown_ref, *rest, pg, nch):
    kt = rest[0:pg]
    vv = rest[pg:2 * pg]
    ckv = rest[2 * pg:3 * pg]
    pt = rest[3 * pg:4 * pg]
    oa_ref, ob_ref = rest[4 * pg:4 * pg + 2]
    lhs_ref, cbf, m_a, l_a, acc_a, m_b, l_b, acc_b = rest[4 * pg + 2:]
    b = pl.program_id(0)
    c = pl.program_id(1)
    nw = MLA_HEADS * MLA_NOPE
    ps = PAGE_SIZE
    kvw = 2 * DA_D

    @pl.when((b == 0) & (c == 0))
    def _():
        lhs_ref[0:nw, :] = wukt_ref[...]

    @pl.when(c == 0)
    def _():
        lhs_ref[nw:nw + 16, :] = qabs_ref[0]
        m_a[...] = jnp.full_like(m_a, MASK_VALUE)
        l_a[...] = jnp.zeros_like(l_a)
        acc_a[...] = jnp.zeros_like(acc_a)
        m_b[...] = jnp.full_like(m_b, MASK_VALUE)
        l_b[...] = jnp.zeros_like(l_b)
        acc_b[...] = jnp.zeros_like(acc_b)

    is_last = (c == nch - 1).astype(F32)
    st_b = (m_b[:, 0:1], l_b[:, 0:1], acc_b[...])
    st_a = [(m_a[n][:, 0:1], l_a[n][:, 0:1], acc_a[n]) for n in range(DA_KV_HEADS)]

    n_tok = pg * ps
    for j in range(pg):
        cbf[j * ps:(j + 1) * ps, :] = ckv[j][...].astype(BF16)
    pe = jnp.concatenate([pt[j][...] for j in range(pg)], axis=1)
    s_rope = jnp.dot(qrope_ref[0], pe.astype(BF16), preferred_element_type=F32)
    ssq_pe = jnp.sum(pe * pe, axis=0, keepdims=True)

    part_a = []
    for n in range(DA_KV_HEADS):
        rows = slice(n * kvw, (n + 1) * kvw)
        ktn = jnp.concatenate([kt[j][rows, :] for j in range(pg)], axis=1).astype(BF16)
        s = jnp.dot(qa_ref[0, n], ktn, preferred_element_type=F32)
        s = jnp.concatenate([s[:, :n_tok - 2 * ps],
                             s[:, n_tok - 2 * ps:] + biasl_ref[n] * is_last], axis=1)
        part_a.append(_local_softmax(s))

    big = lax.dot_general(lhs_ref[...], cbf[...], _NT, preferred_element_type=F32)
    kn = big[0:nw]
    ssq_kn = jnp.sum((kn * kn).reshape(MLA_HEADS, MLA_NOPE, n_tok), axis=1)

    for n in range(DA_KV_HEADS):
        m_k, l_k, p = part_a[n]
        vn = jnp.concatenate([vv[j][pl.ds(n, ps, stride=DA_KV_HEADS), :] for j in range(pg)],
                             axis=0).astype(BF16)
        st_a[n] = _merge_partial(st_a[n], (m_k, l_k, jnp.dot(p, vn, preferred_element_type=F32)))

    rinv = lax.rsqrt((ssq_kn + ssq_pe) * (1.0 / MLA_QK_D) + NORM_EPS)
    s_b = (big[nw:nw + MLA_HEADS] + s_rope[0:MLA_HEADS]) * rinv
    s_b = jnp.concatenate([s_b, jnp.zeros_like(s_b)], axis=0)
    m_k, l_k, p = _local_softmax(s_b)
    st_b = _merge_partial(st_b, (m_k, l_k, jnp.dot(p, cbf[...], preferred_element_type=F32)))

    m_b[...] = jnp.broadcast_to(st_b[0], m_b.shape)
    l_b[...] = jnp.broadcast_to(st_b[1], l_b.shape)
    acc_b[...] = st_b[2]
    for n in range(DA_KV_HEADS):
        m_a[n] = jnp.broadcast_to(st_a[n][0], (16, 128))
        l_a[n] = jnp.broadcast_to(st_a[n][1], (16, 128))
        acc_a[n] = st_a[n][2]

    @pl.when(c == nch - 1)
    def _():
        for n in range(DA_KV_HEADS):
            m_f, l_f, acc_f = _merge_partial(
                st_a[n], (soa_ref[0, n][:, 0:1], jnp.ones((16, 1), F32), voa_ref[0, n]))
            oa_ref[0, n] = acc_f / l_f
        m_f, l_f, acc_f = _merge_partial(
            st_b, (sob_ref[0][:, 0:1], jnp.ones((16, 1), F32), cown_ref[0]))
        ob_ref[0] = acc_f / l_f


def _decode_attention(page_table, qa_mat, q_abs, q_rope, wukt, bias_last,
                      s_own_a, v_own_a, s_own_b, c_own,
                      cache_k, cache_v, cache_c, cache_p):
    B, n_pages = page_table.shape
    pg = DEC_PAGES
    assert n_pages % pg == 0 and pg % 2 == 0
    nch = n_pages // pg
    nw = MLA_HEADS * MLA_NOPE

    def seq_spec(shape):
        nd = len(shape)
        return pl.BlockSpec((1,) + shape, lambda b, c, pt: (b,) + (0,) * nd)

    def const_spec(shape):
        nd = len(shape)
        return pl.BlockSpec(shape, lambda b, c, pt: (0,) * nd)

    def page_spec(arr, j):
        return pl.BlockSpec((None,) + arr.shape[1:],
                            lambda b, c, pt: (pt[b * n_pages + c * pg + j], 0, 0))

    in_specs = [
        seq_spec((DA_KV_HEADS, 16, 2 * DA_D)),
        seq_spec((16, MLA_KV_LORA)),
        seq_spec((16, MLA_ROPE)),
        const_spec((nw, MLA_KV_LORA)),
        const_spec((DA_KV_HEADS, 16, 2 * PAGE_SIZE)),
        seq_spec((DA_KV_HEADS, 16, 128)),
        seq_spec((DA_KV_HEADS, 1, 128)),
        seq_spec((16, 128)),
        seq_spec((1, MLA_KV_LORA)),
    ]
    args = [qa_mat, q_abs, q_rope, wukt, bias_last, s_own_a, v_own_a, s_own_b, c_own]
    for arr in (cache_k, cache_v, cache_c, cache_p):
        for j in range(pg):
            in_specs.append(page_spec(arr, j))
            args.append(arr)
    grid_spec = pltpu.PrefetchScalarGridSpec(
        num_scalar_prefetch=1,
        grid=(B, nch),
        in_specs=in_specs,
        out_specs=[seq_spec((DA_KV_HEADS, 16, 128)), seq_spec((16, MLA_KV_LORA))],
        scratch_shapes=[
            pltpu.VMEM((nw + 16, MLA_KV_LORA), BF16),
            pltpu.VMEM((pg * PAGE_SIZE, MLA_KV_LORA), BF16),
            pltpu.VMEM((DA_KV_HEADS, 16, 128), F32),
            pltpu.VMEM((DA_KV_HEADS, 16, 128), F32),
            pltpu.VMEM((DA_KV_HEADS, 16, 128), F32),
            pltpu.VMEM((16, 128), F32),
            pltpu.VMEM((16, 128), F32),
            pltpu.VMEM((16, MLA_KV_LORA), F32),
        ],
    )
    return pl.pallas_call(
        functools.partial(_decode_kernel, pg=pg, nch=nch),
        grid_spec=grid_spec,
        out_shape=[jax.ShapeDtypeStruct((B, DA_KV_HEADS, 16, 128), F32),
                   jax.ShapeDtypeStruct((B, 16, MLA_KV_LORA), F32)],
        compiler_params=_cparams(("arbitrary", "arbitrary")),
    )(page_table.reshape(-1), *args)


def _moe_kernel(te_ref, tv_ref, x_ref, wg_ref, wu_ref, bg_ref, bu_ref, wd_ref, bd_ref,
                rw_ref, o_ref, acc_ref, *, nf):
    i = pl.program_id(0)
    j = pl.program_id(1)
    valid = tv_ref[i] == 1

    @pl.when(valid)
    def _():
        @pl.when(j == 0)
        def _():
            acc_ref[...] = jnp.zeros_like(acc_ref)

        x = x_ref[...].astype(BF16)
        g = jnp.dot(x, wg_ref[...].astype(BF16), preferred_element_type=F32) + bg_ref[...]
        u = jnp.dot(x, wu_ref[...].astype(BF16), preferred_element_type=F32) + bu_ref[...]
        g = jnp.minimum(g, SWIGLU_LIMIT)
        u = jnp.clip(u, -SWIGLU_LIMIT, SWIGLU_LIMIT)
        act = (u + 1.0) * (g * jax.nn.sigmoid(SWIGLU_ALPHA * g))
        acc_ref[...] += jnp.dot(act.astype(BF16), wd_ref[...].astype(BF16),
                                preferred_element_type=F32)

        @pl.when(j == nf - 1)
        def _():
            o_ref[...] = (acc_ref[...] + bd_ref[...]) * rw_ref[...]

    @pl.when(jnp.logical_not(valid) & (j == nf - 1))
    def _():
        o_ref[...] = jnp.zeros_like(o_ref)


def _moe_ffn(tile_expert, tile_valid, x_sorted, row_w, w_gu, b_gu, w_down, b_down):
    R, D = x_sorted.shape
    E, _, F2 = w_gu.shape
    dff = F2 // 2
    tm, tf = MOE_TM, MOE_TF
    nf = dff // tf
    n_tiles = R // tm
    grid_spec = pltpu.PrefetchScalarGridSpec(
        num_scalar_prefetch=2,
        grid=(n_tiles, nf),
        in_specs=[
            pl.BlockSpec((tm, D), lambda i, j, te, tv: (i, 0)),
            pl.BlockSpec((None, D, tf), lambda i, j, te, tv: (te[i], 0, j)),
            pl.BlockSpec((None, D, tf), lambda i, j, te, tv: (te[i], 0, nf + j)),
            pl.BlockSpec((None, 1, tf), lambda i, j, te, tv: (te[i], 0, j)),
            pl.BlockSpec((None, 1, tf), lambda i, j, te, tv: (te[i], 0, nf + j)),
            pl.BlockSpec((None, tf, D), lambda i, j, te, tv: (te[i], j, 0)),
            pl.BlockSpec((None, 1, D), lambda i, j, te, tv: (te[i], 0, 0)),
            pl.BlockSpec((tm, 1), lambda i, j, te, tv: (i, 0)),
        ],
        out_specs=pl.BlockSpec((tm, D), lambda i, j, te, tv: (i, 0)),
        scratch_shapes=[pltpu.VMEM((tm, D), F32)],
    )
    return pl.pallas_call(
        functools.partial(_moe_kernel, nf=nf),
        grid_spec=grid_spec,
        out_shape=jax.ShapeDtypeStruct((R, D), F32),
        compiler_params=_cparams(("arbitrary", "arbitrary")),
    )(tile_expert, tile_valid, x_sorted, w_gu, w_gu,
      b_gu.reshape(E, 1, F2), b_gu.reshape(E, 1, F2), w_down, b_down.reshape(E, 1, D), row_w)


def _moe(h_parts, w_router, b_router, w_gu, b_gu, w_down, b_down):
    E = w_router.shape[-1]
    tm = MOE_TM
    logits = jnp.concatenate([_mm(h, w_router, b_router, split=True) for h in h_parts], axis=0)
    h_all = jnp.concatenate(h_parts, axis=0)
    N = h_all.shape[0]
    top_val, top_idx = lax.top_k(logits, TOP_K)
    top_w = jax.nn.softmax(top_val, axis=-1)

    P = N * TOP_K
    n_tiles = pl.cdiv(P, tm) + E
    e_flat = top_idx.reshape(P).astype(jnp.int32)
    order = jnp.argsort(e_flat, stable=True).astype(jnp.int32)
    rank = jnp.argsort(order).astype(jnp.int32)
    cnt = jnp.sum(jax.nn.one_hot(e_flat, E, dtype=jnp.int32), axis=0)
    padded = ((cnt + tm - 1) // tm) * tm
    pad_end = jnp.cumsum(padded)
    pad_off = pad_end - padded
    sort_off = jnp.cumsum(cnt) - cnt
    pos = (rank + (pad_off - sort_off)[e_flat]).reshape(N, TOP_K)

    tile_start = jnp.arange(n_tiles, dtype=jnp.int32) * tm
    tile_valid = (tile_start < pad_end[-1]).astype(jnp.int32)
    tile_expert = jnp.minimum(jnp.searchsorted(pad_end, tile_start, side='right'), E - 1)
    last_valid = jnp.maximum(pad_end[-1] // tm - 1, 0)
    tile_expert = jnp.where(tile_valid == 1, tile_expert, tile_expert[last_valid]).astype(jnp.int32)

    in_expert = tile_start[:, None] + jnp.arange(tm, dtype=jnp.int32)[None, :] - pad_off[tile_expert][:, None]
    row_valid = (in_expert < cnt[tile_expert][:, None]) & (tile_valid[:, None] == 1)
    sorted_idx = jnp.clip(sort_off[tile_expert][:, None] + in_expert, 0, P - 1)
    pair = order[sorted_idx.reshape(-1)]
    row_valid = row_valid.reshape(-1)
    src_tok = jnp.where(row_valid, pair // TOP_K, 0)
    row_w = jnp.where(row_valid, top_w.reshape(P)[pair], 0.0)

    y = _moe_ffn(tile_expert, tile_valid, h_all[src_tok], row_w.reshape(-1, 1),
                 w_gu, b_gu, w_down, b_down)
    return jnp.sum(y[pos], axis=1)


def _rms(x, g):
    return x * lax.rsqrt(jnp.mean(x * x, axis=-1, keepdims=True) + NORM_EPS) * g


def _rope(x, pos):
    half = MLA_ROPE // 2
    inv = ROPE_THETA ** (-jnp.arange(half, dtype=F32) / half)
    ang = pos.astype(F32)[:, None] * inv[None, :]
    cos, sin = jnp.cos(ang), jnp.sin(ang)
    if x.ndim == 3:
        cos, sin = cos[:, None, :], sin[:, None, :]
    x1, x2 = x[..., :half], x[..., half:]
    return jnp.concatenate([x1 * cos - x2 * sin, x2 * cos + x1 * sin], axis=-1)


def _rel_bias_by_distance(rel_table, max_n):
    n = jnp.arange(max_n + 1, dtype=jnp.int32)
    max_exact = REL_BUCKETS // 2
    nf = jnp.maximum(n, 1).astype(F32)
    large = max_exact + (jnp.log(nf / max_exact) / math.log(REL_MAX_DIST / max_exact)
                         * (REL_BUCKETS - max_exact)).astype(jnp.int32)
    large = jnp.minimum(large, REL_BUCKETS - 1)
    bucket = jnp.where(n < max_exact, n, large)
    return rel_table[bucket].astype(F32)


def _toeplitz(v, t):
    h = v.shape[0]
    u = v[:, ::-1]
    flat = jnp.tile(u, (1, t))[:, :t * (2 * t - 1)]
    return flat.reshape(h, t, 2 * t - 1)[:, :, t - 1:]


def _split_z(z):
    da_q = DA_HEADS * 2 * DA_D
    da_kv = DA_KV_HEADS * 2 * DA_D
    mla_q = MLA_HEADS * MLA_QK_D
    d_model = (z.shape[-1] - da_q - 2 * da_kv - mla_q - MLA_KV_LORA - MLA_ROPE) // 2
    sizes = (da_q, da_kv, da_kv, mla_q, MLA_KV_LORA, MLA_ROPE, d_model, d_model)
    out, o = [], 0
    for s in sizes:
        out.append(z[:, o:o + s])
        o += s
    return out


def _mixer_inputs(z, pos, g_qa, g_ka, g_qb, g_ckv):
    za_q, za_k, za_v, zb_q, zb_ckv, zb_kpe, g_a, g_b = _split_z(z)
    T = z.shape[0]
    qa = _rms(za_q.reshape(T, DA_KV_HEADS, DA_GROUP, 2, DA_D), g_qa)
    ka = _rms(za_k.reshape(T, DA_KV_HEADS, 2, DA_D), g_ka)
    va = za_v.reshape(T, DA_KV_HEADS, 2 * DA_D)
    qb = zb_q.reshape(T, MLA_HEADS, MLA_QK_D)
    qb = jnp.concatenate([qb[..., :MLA_NOPE], _rope(qb[..., MLA_NOPE:], pos)], axis=-1)
    qb = _rms(qb, g_qb)
    ckv = _rms(zb_ckv, g_ckv)
    kpe = _rope(zb_kpe, pos)
    return qa, ka, va, qb, ckv, kpe, g_a, g_b


def kernel(x_prompt, x_sample, cache_da_k, cache_da_v, cache_mla_ckv, cache_mla_kpe, page_table, c_prompt, c_sample, rel_table, w_ada, b_ada, g_norm1, w_in, g_qa, g_ka, w_lambda, g_oa, g_qb, g_ckv, w_uk, g_kb, w_uv, w_oa, w_ob, w_out, g_norm2, w_router, b_router, w_gu, b_gu, w_down, b_down):
    depth = w_in.shape[0]
    assert depth == 1 and x_prompt.shape[0] == 1 and x_sample.shape[1] == 1
    assert PAGE_SIZE >= REL_MAX_DIST
    layer = 0
    S, D = x_prompt.shape[1], x_prompt.shape[2]
    B = x_sample.shape[0]
    n_pages = page_table.shape[1]
    past_len = n_pages * PAGE_SIZE
    T = ATTN_TILE

    lam_init = 0.8 - 0.6 * math.exp(-0.3 * layer)
    lw = w_lambda[layer]
    lam = jnp.exp(jnp.sum(lw[0] * lw[1])) - jnp.exp(jnp.sum(lw[2] * lw[3])) + lam_init

    xp = x_prompt[0]
    xs = x_sample[:, 0]

    c_all = jnp.concatenate([c_prompt, c_sample], axis=0)
    n_c = c_all.shape[0]
    n_c_pad = -(-n_c // 16) * 16
    c_act = jnp.pad(jax.nn.silu(c_all), ((0, n_c_pad - n_c), (0, 0)))
    mods = _mm(c_act, w_ada[layer], b_ada[layer], tm=n_c_pad, tn=512)[:n_c]
    sh1, sc1, gt1, sh2, sc2, gt2 = [mods[:, i * D:(i + 1) * D] for i in range(6)]

    def modulate(x, g, sc, sh):
        return _rms(x, g) * (1.0 + sc) + sh

    h_p = modulate(xp, g_norm1[layer], sc1[:1], sh1[:1])
    h_s = modulate(xs, g_norm1[layer], sc1[1:], sh1[1:])
    z_p = _mm(h_p.astype(BF16), w_in[layer])
    z_s = _mm(h_s.astype(BF16), w_in[layer])

    pos_p = jnp.arange(S, dtype=jnp.int32)
    pos_s = jnp.full((B,), past_len, jnp.int32)
    qa_p, ka_p, va_p, qb_p, ckv_p, kpe_p, ga_p, gb_p = _mixer_inputs(
        z_p, pos_p, g_qa[layer], g_ka[layer], g_qb[layer], g_ckv[layer])
    qa_s, ka_s, va_s, qb_s, ckv_s, kpe_s, ga_s, gb_s = _mixer_inputs(
        z_s, pos_s, g_qa[layer], g_ka[layer], g_qb[layer], g_ckv[layer])

    tbl = _rel_bias_by_distance(rel_table, REL_MAX_DIST)
    tbl = (tbl - tbl[REL_MAX_DIST][None, :]) * LOG2E

    w_uk_flat = w_uk[layer].reshape(MLA_KV_LORA, MLA_HEADS * MLA_NOPE)
    w_uv_flat = w_uv[layer].reshape(MLA_KV_LORA, MLA_HEADS * MLA_V)
    causal = (jnp.arange(T)[:, None] >= jnp.arange(T)[None, :])[None]
    by_dist = jnp.pad(jnp.transpose(tbl), ((0, 0), (T, T - REL_MAX_DIST - 1)))
    bias_diag = jnp.where(causal, _toeplitz(by_dist, T), MASK_VALUE)
    bias_off = _toeplitz(jnp.pad(by_dist[:, T:], ((0, 0), (0, T))), T)
    bias_da = jnp.stack([bias_diag, bias_off], axis=1)
    mask_b = jnp.stack([jnp.where(causal, 0.0, MASK_VALUE),
                        jnp.zeros((1, T, T), F32)], axis=1)

    eye2 = jnp.eye(2, dtype=F32)

    def da_q_padded(qa):
        q = jnp.transpose(qa, (1, 2, 3, 0, 4)) * (DA_SCALE * LOG2E)
        q = q[:, :, :, :, None, :] * eye2[None, None, :, None, :, None]
        return q.reshape(DA_HEADS * 2, qa.shape[0], 2 * DA_D)

    q_da = da_q_padded(qa_p).astype(BF16)
    k_da = jnp.transpose(ka_p.reshape(S, DA_KV_HEADS, 2 * DA_D), (1, 0, 2)).astype(BF16)
    v_da = jnp.transpose(va_p, (1, 0, 2)).astype(BF16)
    o_da = _flash(q_da, k_da, v_da, bias_da, groups=DA_KV_HEADS, hp=2 * DA_GROUP, kb=1, vb=1,
                  bb=DA_GROUP, kmap=[0] * (2 * DA_GROUP), vmap=[0] * (2 * DA_GROUP),
                  bmap=[c // 2 for c in range(2 * DA_GROUP)], tile=T)
    o_da = o_da.reshape(S, DA_HEADS, 2, 2 * DA_D)
    oa_p = o_da[:, :, 0] - lam * o_da[:, :, 1]

    kn_p = _mm(ckv_p.astype(BF16), w_uk_flat).reshape(S, MLA_HEADS, MLA_NOPE)
    kb_p = jnp.concatenate(
        [kn_p, jnp.broadcast_to(kpe_p[:, None, :], (S, MLA_HEADS, MLA_ROPE))], axis=-1)
    kb_p = _rms(kb_p, g_kb[layer])
    vb_p = _mm(ckv_p.astype(BF16), w_uv_flat).reshape(S, MLA_HEADS, MLA_V)
    q_ml = jnp.transpose(qb_p * (MLA_SCALE * LOG2E), (1, 0, 2)).astype(BF16)
    k_ml = jnp.transpose(kb_p, (1, 0, 2)).astype(BF16)
    v_ml = jnp.transpose(vb_p, (1, 0, 2)).astype(BF16)
    hm = list(range(MLA_HEADS))
    ob_p = _flash(q_ml, k_ml, v_ml, mask_b, groups=1, hp=MLA_HEADS, kb=MLA_HEADS, vb=MLA_HEADS,
                  bb=1, kmap=hm, vmap=hm, bmap=[0] * MLA_HEADS, tile=T)

    qa_mat = jnp.transpose(da_q_padded(qa_s).reshape(DA_KV_HEADS, 2 * DA_GROUP, B, 2 * DA_D),
                           (2, 0, 1, 3))
    qa_mat = jnp.pad(qa_mat, ((0, 0), (0, 0), (0, 16 - 2 * DA_GROUP), (0, 0))).astype(BF16)
    gk = g_kb[layer]
    qn = jnp.transpose(qb_s[..., :MLA_NOPE] * gk[:MLA_NOPE], (1, 0, 2))
    wukt = jnp.transpose(w_uk[layer], (1, 2, 0))
    q_abs = _bmm(qn, wukt, split=True) * (MLA_SCALE * LOG2E)
    q_abs = jnp.pad(jnp.transpose(q_abs, (1, 0, 2)), ((0, 0), (0, 16 - MLA_HEADS), (0, 0))).astype(BF16)
    q_rope = qb_s[..., MLA_NOPE:] * gk[MLA_NOPE:] * (MLA_SCALE * LOG2E)
    q_rope = jnp.pad(q_rope, ((0, 0), (0, 16 - MLA_HEADS), (0, 0))).astype(BF16)
    wukt_flat = wukt.reshape(MLA_HEADS * MLA_NOPE, MLA_KV_LORA).astype(BF16)

    dist_last = PAGE_SIZE - jnp.arange(PAGE_SIZE, dtype=jnp.int32)
    bl = jnp.transpose(tbl[dist_last], (1, 0))
    bl = jnp.repeat(bl.reshape(DA_KV_HEADS, DA_GROUP, 1, PAGE_SIZE), 2, axis=2)
    bl = bl.reshape(DA_KV_HEADS, 2 * DA_GROUP, PAGE_SIZE)
    bias_last = jnp.pad(bl, ((0, 0), (0, 16 - 2 * DA_GROUP), (PAGE_SIZE, 0)))

    s_own_a = jnp.einsum('bngmd,bnmd->bngm', qa_s, ka_s) * (DA_SCALE * LOG2E)
    s_own_a = s_own_a + tbl[0].reshape(1, DA_KV_HEADS, DA_GROUP, 1)
    s_own_a = jnp.pad(s_own_a.reshape(B, DA_KV_HEADS, 2 * DA_GROUP), ((0, 0), (0, 0), (0, 16 - 2 * DA_GROUP)))
    s_own_a = jnp.broadcast_to(s_own_a[..., None], (B, DA_KV_HEADS, 16, 128))
    v_own_a = va_s.reshape(B, DA_KV_HEADS, 1, 2 * DA_D)
    kn_s = _mm(ckv_s.astype(BF16), w_uk_flat, tm=B).reshape(B, MLA_HEADS, MLA_NOPE)
    kb_s = jnp.concatenate(
        [kn_s, jnp.broadcast_to(kpe_s[:, None, :], (B, MLA_HEADS, MLA_ROPE))], axis=-1)
    kb_s = _rms(kb_s, gk)
    s_own_b = jnp.sum(qb_s * kb_s, axis=-1) * (MLA_SCALE * LOG2E)
    s_own_b = jnp.pad(s_own_b, ((0, 0), (0, 16 - MLA_HEADS)))
    s_own_b = jnp.broadcast_to(s_own_b[..., None], (B, 16, 128))
    c_own = ckv_s.reshape(B, 1, MLA_KV_LORA)

    n_phys = cache_da_k.shape[1]
    ra, rb = _decode_attention(
        page_table, qa_mat, q_abs, q_rope, wukt_flat, bias_last,
        s_own_a, v_own_a, s_own_b, c_own,
        jnp.transpose(cache_da_k[layer], (0, 2, 3, 4, 1)).reshape(n_phys, DA_KV_HEADS * 2 * DA_D, PAGE_SIZE),
        cache_da_v[layer].reshape(n_phys, PAGE_SIZE * DA_KV_HEADS, 2 * DA_D),
        cache_mla_ckv[layer],
        jnp.transpose(cache_mla_kpe[layer], (0, 2, 1)))
    ra = ra[:, :, :2 * DA_GROUP].reshape(B, DA_HEADS, 2, 2 * DA_D)
    oa_s = ra[:, :, 0] - lam * ra[:, :, 1]
    rb_h = jnp.transpose(rb[:, :MLA_HEADS], (1, 0, 2)).astype(BF16)
    wuv_h = jnp.transpose(w_uv[layer], (1, 0, 2))
    ob_s = jnp.transpose(_bmm(rb_h, wuv_h, tm=B), (1, 0, 2)).reshape(B, MLA_HEADS * MLA_V)

    def post_attention(x, oa, ob, g_a, g_b, rows):
        oa = (_rms(oa, g_oa[layer]) * (1.0 - lam_init)).reshape(x.shape[0], DA_HEADS * 2 * DA_D)
        ya = _mm(oa.astype(BF16), w_oa[layer])
        yb = _mm(ob.astype(BF16), w_ob[layer])
        mix = (jax.nn.sigmoid(g_a) * ya + jax.nn.sigmoid(g_b) * yb).astype(BF16)
        x1 = x + gt1[rows] * _mm(mix, w_out[layer])
        h2 = _rms(x1, g_norm2[layer]) * (1.0 + sc2[rows]) + sh2[rows]
        return x1, h2

    p_rows, s_rows = slice(0, 1), slice(1, None)
    x1_p, h2_p = post_attention(xp, oa_p, ob_p, ga_p, gb_p, p_rows)
    x1_s, h2_s = post_attention(xs, oa_s, ob_s, ga_s, gb_s, s_rows)
    y_moe = _moe([h2_p, h2_s], w_router[layer], b_router[layer], w_gu[layer], b_gu[layer],
                 w_down[layer], b_down[layer])
    y_prompt = (x1_p + gt2[p_rows] * y_moe[:S])[None]
    y_sample = (x1_s + gt2[s_rows] * y_moe[S:])[:, None]
    return (y_prompt, y_sample,
            ka_p[None, None], va_p[None, None], ckv_p[None, None], kpe_p[None, None],
            ka_s[None, :, None], va_s[None, :, None], ckv_s[None, :, None], kpe_s[None, :, None])
```

```python
import functools
import math

import jax
import jax.numpy as jnp
from jax import lax
from jax.experimental import pallas as pl
from jax.experimental.pallas import tpu as pltpu

F32 = jnp.float32
BF16 = jnp.bfloat16

PAGE_SIZE = 128
DA_HEADS = 8
DA_KV_HEADS = 2
DA_GROUP = DA_HEADS // DA_KV_HEADS
DA_D = 64
DA_SCALE = 1.0 / math.sqrt(DA_D)
MLA_HEADS = 8
MLA_NOPE = 128
MLA_ROPE = 64
MLA_V = 128
MLA_KV_LORA = 512
MLA_QK_D = MLA_NOPE + MLA_ROPE
MLA_SCALE = 1.0 / math.sqrt(MLA_QK_D)
ROPE_THETA = 10000.0
REL_BUCKETS = 32
REL_MAX_DIST = 128
N_EXPERTS = 32
TOP_K = 4
SWIGLU_LIMIT = 7.0
SWIGLU_ALPHA = 1.702
NORM_EPS = 1e-6

LOG2E = math.log2(math.e)
MASK_VALUE = -1e30
VMEM_LIMIT = 56 * 1024 * 1024

ATTN_TILE = 512
MLA_ATTN_TILE = 1024
MLA_GROUPS = 2
DEC_PAGES = 16
MOE_TM = 576
MOE_TF = 512

_NT = (((1,), (1,)), ((), ()))


def _cparams(sem):
    return pltpu.CompilerParams(dimension_semantics=sem, vmem_limit_bytes=VMEM_LIMIT)


def _split_bf16(x):
    hi = x.astype(BF16)
    lo = (x - hi.astype(F32)).astype(BF16)
    return hi, lo


def _mm_kernel(*refs, nk, split, has_bias):
    if has_bias:
        a_ref, b_ref, bias_ref, o_ref, acc_ref = refs
    else:
        a_ref, b_ref, o_ref, acc_ref = refs
        bias_ref = None
    k = pl.program_id(3)

    @pl.when(k == 0)
    def _():
        acc_ref[...] = jnp.zeros_like(acc_ref)

    if split:
        a_hi, a_lo = _split_bf16(a_ref[...].astype(F32))
        b_hi, b_lo = _split_bf16(b_ref[...].astype(F32))
        part = jnp.dot(a_hi, b_hi, preferred_element_type=F32)
        part += jnp.dot(a_lo, b_hi, preferred_element_type=F32)
        part += jnp.dot(a_hi, b_lo, preferred_element_type=F32)
    else:
        part = jnp.dot(a_ref[...].astype(BF16), b_ref[...].astype(BF16),
                       preferred_element_type=F32)
    acc_ref[...] += part

    @pl.when(k == nk - 1)
    def _():
        r = acc_ref[...]
        if has_bias:
            r = r + bias_ref[...]
        o_ref[...] = r.astype(o_ref.dtype)


def _pick(n, cands):
    for c in cands:
        if n % c == 0:
            return c
    return n


def _bmm(a, b, bias=None, *, out_dtype=F32, split=False, tm=None, tn=None, tk=None):
    G, M, K = a.shape
    _, _, N = b.shape
    tm = tm or _pick(M, (640, 512, 256, 128))
    tk = tk or K
    tn = tn or (N if N <= 1024 else 1024)
    nm, nn, nk = M // tm, pl.cdiv(N, tn), K // tk
    assert M % tm == 0 and K % tk == 0
    in_specs = [
        pl.BlockSpec((None, tm, tk), lambda g, j, i, k: (g, i, k)),
        pl.BlockSpec((None, tk, tn), lambda g, j, i, k: (g, k, j)),
    ]
    args = [a, b]
    if bias is not None:
        in_specs.append(pl.BlockSpec((None, 1, tn), lambda g, j, i, k: (g, 0, j)))
        args.append(bias.reshape(G, 1, N).astype(F32))
    return pl.pallas_call(
        functools.partial(_mm_kernel, nk=nk, split=split, has_bias=bias is not None),
        grid=(G, nn, nm, nk),
        in_specs=in_specs,
        out_specs=pl.BlockSpec((None, tm, tn), lambda g, j, i, k: (g, i, j)),
        out_shape=jax.ShapeDtypeStruct((G, M, N), out_dtype),
        scratch_shapes=[pltpu.VMEM((tm, tn), F32)],
        compiler_params=_cparams(("parallel", "parallel", "parallel", "arbitrary")),
    )(*args)


def _mm(a, b, bias=None, **kw):
    return _bmm(a[None], b[None], None if bias is None else bias[None], **kw)[0]


def _flash_kernel(qi_ref, ki_ref, q_ref, k_ref, v_ref, bias_ref, o_ref,
                  m_ref, l_ref, acc_ref, *, hp, kmap, vmap, bmap, dv):
    t = pl.program_id(1)
    qi = qi_ref[t]
    ki = ki_ref[t]

    @pl.when(ki == 0)
    def _():
        m_ref[...] = jnp.full_like(m_ref, MASK_VALUE)
        l_ref[...] = jnp.zeros_like(l_ref)
        acc_ref[...] = jnp.zeros_like(acc_ref)

    def update(near):
        for c in range(hp):
            s = lax.dot_general(q_ref[c], k_ref[kmap[c]], _NT, preferred_element_type=F32)
            if near:
                s = s + bias_ref[bmap[c], qi - ki]
            m_prev = m_ref[c]
            m_new = jnp.maximum(m_prev, jnp.max(s, axis=1, keepdims=True))
            alpha = jnp.exp2(m_prev - m_new)
            p = jnp.exp2(s - m_new[:, :1])
            psum = p[:, 0:128]
            for w in range(1, p.shape[1] // 128):
                psum = psum + p[:, w * 128:(w + 1) * 128]
            l_ref[c] = alpha * l_ref[c] + psum
            acc_ref[c] = alpha[:, :dv] * acc_ref[c] + jnp.dot(
                p.astype(BF16), v_ref[vmap[c]], preferred_element_type=F32)
            m_ref[c] = m_new

    @pl.when(qi - ki <= 1)
    def _():
        update(True)

    @pl.when(qi - ki > 1)
    def _():
        update(False)

    @pl.when(ki == qi)
    def _():
        for c in range(hp):
            l_row = jnp.sum(l_ref[c], axis=1, keepdims=True)
            o_ref[:, c * dv:(c + 1) * dv] = (acc_ref[c] / l_row).astype(o_ref.dtype)


def _flash(q, k, v, bias, *, groups, hp, kb, vb, bb, kmap, vmap, bmap, tile):
    _, S, dk = q.shape
    dv = v.shape[-1]
    T = tile
    assert dv == 128 and S % T == 0
    nq = S // T
    qi_l, ki_l = [], []
    for i in range(nq):
        for j in range(i + 1):
            qi_l.append(i)
            ki_l.append(j)
    qi_arr = jnp.asarray(qi_l, jnp.int32)
    ki_arr = jnp.asarray(ki_l, jnp.int32)
    grid_spec = pltpu.PrefetchScalarGridSpec(
        num_scalar_prefetch=2,
        grid=(groups, len(qi_l)),
        in_specs=[
            pl.BlockSpec((hp, T, dk), lambda g, t, qi, ki: (g, qi[t], 0)),
            pl.BlockSpec((kb, T, dk), lambda g, t, qi, ki: (g, ki[t], 0)),
            pl.BlockSpec((vb, T, dv), lambda g, t, qi, ki: (g, ki[t], 0)),
            pl.BlockSpec((bb, 2, T, T), lambda g, t, qi, ki: (g, 0, 0, 0)),
        ],
        out_specs=pl.BlockSpec((T, hp * dv), lambda g, t, qi, ki: (qi[t], g)),
        scratch_shapes=[pltpu.VMEM((hp, T, 128), F32),
                        pltpu.VMEM((hp, T, 128), F32),
                        pltpu.VMEM((hp, T, dv), F32)],
    )
    return pl.pallas_call(
        functools.partial(_flash_kernel, hp=hp, kmap=kmap, vmap=vmap, bmap=bmap, dv=dv),
        grid_spec=grid_spec,
        out_shape=jax.ShapeDtypeStruct((S, groups * hp * dv), F32),
        compiler_params=_cparams(("parallel", "arbitrary")),
    )(qi_arr, ki_arr, q, k, v, bias)


def _merge_partial(state, part):
    m, l, acc = state
    m_k, l_k, acc_k = part
    m_new = jnp.maximum(m, m_k)
    a = jnp.exp2(m - m_new)
    b = jnp.exp2(m_k - m_new)
    return m_new, a * l + b * l_k, a * acc + b * acc_k


def _local_softmax(s):
    m_k = jnp.max(s, axis=1, keepdims=True)
    p = jnp.exp2(s - m_k)
    return m_k, jnp.sum(p, axis=1, keepdims=True), p.astype(BF16)


def _decode_kernel(pt_ref, qa_ref, qabs_ref, qrope_ref, wukt_ref, biasl_ref,
                   soa_ref, voa_ref, sob_ref, cown_ref, *rest, pg, nch):
    kt = rest[0:pg]
    vv = rest[pg:2 * pg]
    ckv = rest[2 * pg:3 * pg]
    pt = rest[3 * pg:4 * pg]
    oa_ref, ob_ref = rest[4 * pg:4 * pg + 2]
    lhs_ref, cbf, m_a, l_a, acc_a, m_b, l_b, acc_b = rest[4 * pg + 2:]
    b = pl.program_id(0)
    c = pl.program_id(1)
    nw = MLA_HEADS * MLA_NOPE
    ps = PAGE_SIZE
    kvw = 2 * DA_D

    @pl.when((b == 0) & (c == 0))
    def _():
        lhs_ref[0:nw, :] = wukt_ref[...]

    @pl.when(c == 0)
    def _():
        lhs_ref[nw:nw + 16, :] = qabs_ref[0]
        m_a[...] = jnp.full_like(m_a, MASK_VALUE)
        l_a[...] = jnp.zeros_like(l_a)
        acc_a[...] = jnp.zeros_like(acc_a)
        m_b[...] = jnp.full_like(m_b, MASK_VALUE)
        l_b[...] = jnp.zeros_like(l_b)
        acc_b[...] = jnp.zeros_like(acc_b)

    is_last = (c == nch - 1).astype(F32)
    st_b = (m_b[:, 0:1], l_b[:, 0:1], acc_b[...])
    st_a = [(m_a[n][:, 0:1], l_a[n][:, 0:1], acc_a[n]) for n in range(DA_KV_HEADS)]

    n_tok = pg * ps
    for j in range(pg):
        cbf[j * ps:(j + 1) * ps, :] = ckv[j][...].astype(BF16)
    pe = jnp.concatenate([pt[j][...] for j in range(pg)], axis=1)
    s_rope = jnp.dot(qrope_ref[0], pe.astype(BF16), preferred_element_type=F32)
    ssq_pe = jnp.sum(pe * pe, axis=0, keepdims=True)

    part_a = []
    for n in range(DA_KV_HEADS):
        rows = slice(n * kvw, (n + 1) * kvw)
        ktn = jnp.concatenate([kt[j][rows, :] for j in range(pg)], axis=1).astype(BF16)
        s = jnp.dot(qa_ref[0, n], ktn, preferred_element_type=F32)
        s = jnp.concatenate([s[:, :n_tok - 2 * ps],
                             s[:, n_tok - 2 * ps:] + biasl_ref[n] * is_last], axis=1)
        part_a.append(_local_softmax(s))

    big = lax.dot_general(lhs_ref[...], cbf[...], _NT, preferred_element_type=F32)
    kn = big[0:nw]
    ssq_kn = jnp.sum((kn * kn).reshape(MLA_HEADS, MLA_NOPE, n_tok), axis=1)

    for n in range(DA_KV_HEADS):
        m_k, l_k, p = part_a[n]
        vn = jnp.concatenate([vv[j][pl.ds(n, ps, stride=DA_KV_HEADS), :] for j in range(pg)],
                             axis=0).astype(BF16)
        st_a[n] = _merge_partial(st_a[n], (m_k, l_k, jnp.dot(p, vn, preferred_element_type=F32)))

    rinv = lax.rsqrt((ssq_kn + ssq_pe) * (1.0 / MLA_QK_D) + NORM_EPS)
    s_b = (big[nw:nw + MLA_HEADS] + s_rope[0:MLA_HEADS]) * rinv
    s_b = jnp.concatenate([s_b, jnp.zeros_like(s_b)], axis=0)
    m_k, l_k, p = _local_softmax(s_b)
    st_b = _merge_partial(st_b, (m_k, l_k, jnp.dot(p, cbf[...], preferred_element_type=F32)))

    m_b[...] = jnp.broadcast_to(st_b[0], m_b.shape)
    l_b[...] = jnp.broadcast_to(st_b[1], l_b.shape)
    acc_b[...] = st_b[2]
    for n in range(DA_KV_HEADS):
        m_a[n] = jnp.broadcast_to(st_a[n][0], (16, 128))
        l_a[n] = jnp.broadcast_to(st_a[n][1], (16, 128))
        acc_a[n] = st_a[n][2]

    @pl.when(c == nch - 1)
    def _():
        for n in range(DA_KV_HEADS):
            m_f, l_f, acc_f = _merge_partial(
                st_a[n], (soa_ref[0, n][:, 0:1], jnp.ones((16, 1), F32), voa_ref[0, n]))
            oa_ref[0, n] = acc_f / l_f
        m_f, l_f, acc_f = _merge_partial(
            st_b, (sob_ref[0][:, 0:1], jnp.ones((16, 1), F32), cown_ref[0]))
        ob_ref[0] = acc_f / l_f


def _decode_attention(page_table, qa_mat, q_abs, q_rope, wukt, bias_last,
                      s_own_a, v_own_a, s_own_b, c_own,
                      cache_k, cache_v, cache_c, cache_p):
    B, n_pages = page_table.shape
    pg = DEC_PAGES
    assert n_pages % pg == 0 and pg % 2 == 0
    nch = n_pages // pg
    nw = MLA_HEADS * MLA_NOPE

    def seq_spec(shape):
        nd = len(shape)
        return pl.BlockSpec((1,) + shape, lambda b, c, pt: (b,) + (0,) * nd)

    def const_spec(shape):
        nd = len(shape)
        return pl.BlockSpec(shape, lambda b, c, pt: (0,) * nd)

    def page_spec(arr, j):
        return pl.BlockSpec((None,) + arr.shape[1:],
                            lambda b, c, pt: (pt[b * n_pages + c * pg + j], 0, 0))

    in_specs = [
        seq_spec((DA_KV_HEADS, 16, 2 * DA_D)),
        seq_spec((16, MLA_KV_LORA)),
        seq_spec((16, MLA_ROPE)),
        const_spec((nw, MLA_KV_LORA)),
        const_spec((DA_KV_HEADS, 16, 2 * PAGE_SIZE)),
        seq_spec((DA_KV_HEADS, 16, 128)),
        seq_spec((DA_KV_HEADS, 1, 128)),
        seq_spec((16, 128)),
        seq_spec((1, MLA_KV_LORA)),
    ]
    args = [qa_mat, q_abs, q_rope, wukt, bias_last, s_own_a, v_own_a, s_own_b, c_own]
    for arr in (cache_k, cache_v, cache_c, cache_p):
        for j in range(pg):
            in_specs.append(page_spec(arr, j))
            args.append(arr)
    grid_spec = pltpu.PrefetchScalarGridSpec(
        num_scalar_prefetch=1,
        grid=(B, nch),
        in_specs=in_specs,
        out_specs=[seq_spec((DA_KV_HEADS, 16, 128)), seq_spec((16, MLA_KV_LORA))],
        scratch_shapes=[
            pltpu.VMEM((nw + 16, MLA_KV_LORA), BF16),
            pltpu.VMEM((pg * PAGE_SIZE, MLA_KV_LORA), BF16),
            pltpu.VMEM((DA_KV_HEADS, 16, 128), F32),
            pltpu.VMEM((DA_KV_HEADS, 16, 128), F32),
            pltpu.VMEM((DA_KV_HEADS, 16, 128), F32),
            pltpu.VMEM((16, 128), F32),
            pltpu.VMEM((16, 128), F32),
            pltpu.VMEM((16, MLA_KV_LORA), F32),
        ],
    )
    return pl.pallas_call(
        functools.partial(_decode_kernel, pg=pg, nch=nch),
        grid_spec=grid_spec,
        out_shape=[jax.ShapeDtypeStruct((B, DA_KV_HEADS, 16, 128), F32),
                   jax.ShapeDtypeStruct((B, 16, MLA_KV_LORA), F32)],
        compiler_params=_cparams(("arbitrary", "arbitrary")),
    )(page_table.reshape(-1), *args)


def _moe_kernel(te_ref, tv_ref, x_ref, wg_ref, wu_ref, bg_ref, bu_ref, wd_ref, bd_ref,
                rw_ref, o_ref, acc_ref, *, nf):
    i = pl.program_id(0)
    j = pl.program_id(1)
    valid = tv_ref[i] == 1

    @pl.when(valid)
    def _():
        @pl.when(j == 0)
        def _():
            acc_ref[...] = jnp.zeros_like(acc_ref)

        x = x_ref[...].astype(BF16)
        g = jnp.dot(x, wg_ref[...].astype(BF16), preferred_element_type=F32) + bg_ref[...]
        u = jnp.dot(x, wu_ref[...].astype(BF16), preferred_element_type=F32) + bu_ref[...]
        g = jnp.minimum(g, SWIGLU_LIMIT)
        u = jnp.clip(u, -SWIGLU_LIMIT, SWIGLU_LIMIT)
        act = (u + 1.0) * (g * jax.nn.sigmoid(SWIGLU_ALPHA * g))
        acc_ref[...] += jnp.dot(act.astype(BF16), wd_ref[...].astype(BF16),
                                preferred_element_type=F32)

        @pl.when(j == nf - 1)
        def _():
            o_ref[...] = (acc_ref[...] + bd_ref[...]) * rw_ref[...]

    @pl.when(jnp.logical_not(valid) & (j == nf - 1))
    def _():
        o_ref[...] = jnp.zeros_like(o_ref)


def _moe_ffn(tile_expert, tile_valid, x_sorted, row_w, w_gu, b_gu, w_down, b_down):
    R, D = x_sorted.shape
    E, _, F2 = w_gu.shape
    dff = F2 // 2
    tm, tf = MOE_TM, MOE_TF
    nf = dff // tf
    n_tiles = R // tm
    grid_spec = pltpu.PrefetchScalarGridSpec(
        num_scalar_prefetch=2,
        grid=(n_tiles, nf),
        in_specs=[
            pl.BlockSpec((tm, D), lambda i, j, te, tv: (i, 0)),
            pl.BlockSpec((None, D, tf), lambda i, j, te, tv: (te[i], 0, j)),
            pl.BlockSpec((None, D, tf), lambda i, j, te, tv: (te[i], 0, nf + j)),
            pl.BlockSpec((None, 1, tf), lambda i, j, te, tv: (te[i], 0, j)),
            pl.BlockSpec((None, 1, tf), lambda i, j, te, tv: (te[i], 0, nf + j)),
            pl.BlockSpec((None, tf, D), lambda i, j, te, tv: (te[i], j, 0)),
            pl.BlockSpec((None, 1, D), lambda i, j, te, tv: (te[i], 0, 0)),
            pl.BlockSpec((tm, 1), lambda i, j, te, tv: (i, 0)),
        ],
        out_specs=pl.BlockSpec((tm, D), lambda i, j, te, tv: (i, 0)),
        scratch_shapes=[pltpu.VMEM((tm, D), F32)],
    )
    return pl.pallas_call(
        functools.partial(_moe_kernel, nf=nf),
        grid_spec=grid_spec,
        out_shape=jax.ShapeDtypeStruct((R, D), F32),
        compiler_params=_cparams(("arbitrary", "arbitrary")),
    )(tile_expert, tile_valid, x_sorted, w_gu, w_gu,
      b_gu.reshape(E, 1, F2), b_gu.reshape(E, 1, F2), w_down, b_down.reshape(E, 1, D), row_w)


def _moe(h_parts, w_router, b_router, w_gu, b_gu, w_down, b_down):
    E = w_router.shape[-1]
    tm = MOE_TM
    logits = jnp.concatenate([_mm(h, w_router, b_router, split=True) for h in h_parts], axis=0)
    h_all = jnp.concatenate(h_parts, axis=0)
    N = h_all.shape[0]
    top_val, top_idx = lax.top_k(logits, TOP_K)
    top_w = jax.nn.softmax(top_val, axis=-1)

    P = N * TOP_K
    n_tiles = pl.cdiv(P, tm) + E
    e_flat = top_idx.reshape(P).astype(jnp.int32)
    order = jnp.argsort(e_flat, stable=True).astype(jnp.int32)
    rank = jnp.argsort(order).astype(jnp.int32)
    cnt = jnp.sum(jax.nn.one_hot(e_flat, E, dtype=jnp.int32), axis=0)
    padded = ((cnt + tm - 1) // tm) * tm
    pad_end = jnp.cumsum(padded)
    pad_off = pad_end - padded
    sort_off = jnp.cumsum(cnt) - cnt
    pos = (rank + (pad_off - sort_off)[e_flat]).reshape(N, TOP_K)

    tile_start = jnp.arange(n_tiles, dtype=jnp.int32) * tm
    tile_valid = (tile_start < pad_end[-1]).astype(jnp.int32)
    tile_expert = jnp.minimum(jnp.searchsorted(pad_end, tile_start, side='right'), E - 1)
    last_valid = jnp.maximum(pad_end[-1] // tm - 1, 0)
    tile_expert = jnp.where(tile_valid == 1, tile_expert, tile_expert[last_valid]).astype(jnp.int32)

    in_expert = tile_start[:, None] + jnp.arange(tm, dtype=jnp.int32)[None, :] - pad_off[tile_expert][:, None]
    row_valid = (in_expert < cnt[tile_expert][:, None]) & (tile_valid[:, None] == 1)
    sorted_idx = jnp.clip(sort_off[tile_expert][:, None] + in_expert, 0, P - 1)
    pair = order[sorted_idx.reshape(-1)]
    row_valid = row_valid.reshape(-1)
    src_tok = jnp.where(row_valid, pair // TOP_K, 0)
    row_w = jnp.where(row_valid, top_w.reshape(P)[pair], 0.0)

    y = _moe_ffn(tile_expert, tile_valid, h_all[src_tok], row_w.reshape(-1, 1),
                 w_gu, b_gu, w_down, b_down)
    out = y[pos[:, 0]]
    for k in range(1, TOP_K):
        out = out + y[pos[:, k]]
    return out


def _rms(x, g):
    return x * lax.rsqrt(jnp.mean(x * x, axis=-1, keepdims=True) + NORM_EPS) * g


def _rope(x, pos):
    half = MLA_ROPE // 2
    inv = ROPE_THETA ** (-jnp.arange(half, dtype=F32) / half)
    ang = pos.astype(F32)[:, None] * inv[None, :]
    cos, sin = jnp.cos(ang), jnp.sin(ang)
    if x.ndim == 3:
        cos, sin = cos[:, None, :], sin[:, None, :]
    x1, x2 = x[..., :half], x[..., half:]
    return jnp.concatenate([x1 * cos - x2 * sin, x2 * cos + x1 * sin], axis=-1)


def _rel_bias_by_distance(rel_table, max_n):
    n = jnp.arange(max_n + 1, dtype=jnp.int32)
    max_exact = REL_BUCKETS // 2
    nf = jnp.maximum(n, 1).astype(F32)
    large = max_exact + (jnp.log(nf / max_exact) / math.log(REL_MAX_DIST / max_exact)
                         * (REL_BUCKETS - max_exact)).astype(jnp.int32)
    large = jnp.minimum(large, REL_BUCKETS - 1)
    bucket = jnp.where(n < max_exact, n, large)
    return rel_table[bucket].astype(F32)


def _toeplitz(v, t):
    h = v.shape[0]
    u = v[:, ::-1]
    flat = jnp.tile(u, (1, t))[:, :t * (2 * t - 1)]
    return flat.reshape(h, t, 2 * t - 1)[:, :, t - 1:]


def _split_z(z):
    da_q = DA_HEADS * 2 * DA_D
    da_kv = DA_KV_HEADS * 2 * DA_D
    mla_q = MLA_HEADS * MLA_QK_D
    d_model = (z.shape[-1] - da_q - 2 * da_kv - mla_q - MLA_KV_LORA - MLA_ROPE) // 2
    sizes = (da_q, da_kv, da_kv, mla_q, MLA_KV_LORA, MLA_ROPE, d_model, d_model)
    out, o = [], 0
    for s in sizes:
        out.append(z[:, o:o + s])
        o += s
    return out


def _mixer_inputs(z, pos, g_qa, g_ka, g_qb, g_ckv):
    za_q, za_k, za_v, zb_q, zb_ckv, zb_kpe, g_a, g_b = _split_z(z)
    T = z.shape[0]
    qa = _rms(za_q.reshape(T, DA_KV_HEADS, DA_GROUP, 2, DA_D), g_qa)
    ka = _rms(za_k.reshape(T, DA_KV_HEADS, 2, DA_D), g_ka)
    va = za_v.reshape(T, DA_KV_HEADS, 2 * DA_D)
    qb = zb_q.reshape(T, MLA_HEADS, MLA_QK_D)
    qb = jnp.concatenate([qb[..., :MLA_NOPE], _rope(qb[..., MLA_NOPE:], pos)], axis=-1)
    qb = _rms(qb, g_qb)
    ckv = _rms(zb_ckv, g_ckv)
    kpe = _rope(zb_kpe, pos)
    return qa, ka, va, qb, ckv, kpe, g_a, g_b


def kernel(x_prompt, x_sample, cache_da_k, cache_da_v, cache_mla_ckv, cache_mla_kpe, page_table, c_prompt, c_sample, rel_table, w_ada, b_ada, g_norm1, w_in, g_qa, g_ka, w_lambda, g_oa, g_qb, g_ckv, w_uk, g_kb, w_uv, w_oa, w_ob, w_out, g_norm2, w_router, b_router, w_gu, b_gu, w_down, b_down):
    depth = w_in.shape[0]
    assert depth == 1 and x_prompt.shape[0] == 1 and x_sample.shape[1] == 1
    assert PAGE_SIZE >= REL_MAX_DIST
    layer = 0
    S, D = x_prompt.shape[1], x_prompt.shape[2]
    B = x_sample.shape[0]
    n_pages = page_table.shape[1]
    past_len = n_pages * PAGE_SIZE
    T = ATTN_TILE

    lam_init = 0.8 - 0.6 * math.exp(-0.3 * layer)
    lw = w_lambda[layer]
    lam = jnp.exp(jnp.sum(lw[0] * lw[1])) - jnp.exp(jnp.sum(lw[2] * lw[3])) + lam_init

    xp = x_prompt[0]
    xs = x_sample[:, 0]

    c_all = jnp.concatenate([c_prompt, c_sample], axis=0)
    n_c = c_all.shape[0]
    n_c_pad = -(-n_c // 16) * 16
    c_act = jnp.pad(jax.nn.silu(c_all), ((0, n_c_pad - n_c), (0, 0)))
    mods = _mm(c_act, w_ada[layer], b_ada[layer], tm=n_c_pad, tn=512)[:n_c]
    sh1, sc1, gt1, sh2, sc2, gt2 = [mods[:, i * D:(i + 1) * D] for i in range(6)]

    def modulate(x, g, sc, sh):
        return _rms(x, g) * (1.0 + sc) + sh

    h_p = modulate(xp, g_norm1[layer], sc1[:1], sh1[:1])
    h_s = modulate(xs, g_norm1[layer], sc1[1:], sh1[1:])
    z_p = _mm(h_p.astype(BF16), w_in[layer])
    z_s = _mm(h_s.astype(BF16), w_in[layer])

    pos_p = jnp.arange(S, dtype=jnp.int32)
    pos_s = jnp.full((B,), past_len, jnp.int32)
    qa_p, ka_p, va_p, qb_p, ckv_p, kpe_p, ga_p, gb_p = _mixer_inputs(
        z_p, pos_p, g_qa[layer], g_ka[layer], g_qb[layer], g_ckv[layer])
    qa_s, ka_s, va_s, qb_s, ckv_s, kpe_s, ga_s, gb_s = _mixer_inputs(
        z_s, pos_s, g_qa[layer], g_ka[layer], g_qb[layer], g_ckv[layer])

    tbl = _rel_bias_by_distance(rel_table, REL_MAX_DIST)
    tbl = (tbl - tbl[REL_MAX_DIST][None, :]) * LOG2E

    w_uk_flat = w_uk[layer].reshape(MLA_KV_LORA, MLA_HEADS * MLA_NOPE)
    w_uv_flat = w_uv[layer].reshape(MLA_KV_LORA, MLA_HEADS * MLA_V)
    causal = (jnp.arange(T)[:, None] >= jnp.arange(T)[None, :])[None]
    by_dist = jnp.pad(jnp.transpose(tbl), ((0, 0), (T, T - REL_MAX_DIST - 1)))
    bias_diag = jnp.where(causal, _toeplitz(by_dist, T), MASK_VALUE)
    bias_off = _toeplitz(jnp.pad(by_dist[:, T:], ((0, 0), (0, T))), T)
    bias_da = jnp.stack([bias_diag, bias_off], axis=1)
    tm_ = MLA_ATTN_TILE if S % MLA_ATTN_TILE == 0 else T
    causal_m = (jnp.arange(tm_)[:, None] >= jnp.arange(tm_)[None, :])[None]
    mask_b = jnp.stack([jnp.where(causal_m, 0.0, MASK_VALUE),
                        jnp.zeros((1, tm_, tm_), F32)], axis=1)
    mask_b = jnp.tile(mask_b, (MLA_GROUPS, 1, 1, 1))

    eye2 = jnp.eye(2, dtype=F32)

    def da_q_padded(qa):
        q = jnp.transpose(qa, (1, 2, 3, 0, 4)) * (DA_SCALE * LOG2E)
        q = q[:, :, :, :, None, :] * eye2[None, None, :, None, :, None]
        return q.reshape(DA_HEADS * 2, qa.shape[0], 2 * DA_D)

    q_da = da_q_padded(qa_p).astype(BF16)
    k_da = jnp.transpose(ka_p.reshape(S, DA_KV_HEADS, 2 * DA_D), (1, 0, 2)).astype(BF16)
    v_da = jnp.transpose(va_p, (1, 0, 2)).astype(BF16)
    o_da = _flash(q_da, k_da, v_da, bias_da, groups=DA_KV_HEADS, hp=2 * DA_GROUP, kb=1, vb=1,
                  bb=DA_GROUP, kmap=[0] * (2 * DA_GROUP), vmap=[0] * (2 * DA_GROUP),
                  bmap=[c // 2 for c in range(2 * DA_GROUP)], tile=T)
    o_da = o_da.reshape(S, DA_HEADS, 2, 2 * DA_D)
    oa_p = o_da[:, :, 0] - lam * o_da[:, :, 1]

    kn_p = _mm(ckv_p.astype(BF16), w_uk_flat).reshape(S, MLA_HEADS, MLA_NOPE)
    kb_p = jnp.concatenate(
        [kn_p, jnp.broadcast_to(kpe_p[:, None, :], (S, MLA_HEADS, MLA_ROPE))], axis=-1)
    kb_p = _rms(kb_p, g_kb[layer])
    vb_p = _mm(ckv_p.astype(BF16), w_uv_flat).reshape(S, MLA_HEADS, MLA_V)
    q_ml = jnp.transpose(qb_p * (MLA_SCALE * LOG2E), (1, 0, 2)).astype(BF16)
    k_ml = jnp.transpose(kb_p, (1, 0, 2)).astype(BF16)
    v_ml = jnp.transpose(vb_p, (1, 0, 2)).astype(BF16)
    hpm = MLA_HEADS // MLA_GROUPS
    hm = list(range(hpm))
    ob_p = _flash(q_ml, k_ml, v_ml, mask_b, groups=MLA_GROUPS, hp=hpm, kb=hpm, vb=hpm,
                  bb=1, kmap=hm, vmap=hm, bmap=[0] * hpm, tile=tm_)

    qa_mat = jnp.transpose(da_q_padded(qa_s).reshape(DA_KV_HEADS, 2 * DA_GROUP, B, 2 * DA_D),
                           (2, 0, 1, 3))
    qa_mat = jnp.pad(qa_mat, ((0, 0), (0, 0), (0, 16 - 2 * DA_GROUP), (0, 0))).astype(BF16)
    gk = g_kb[layer]
    qn = jnp.transpose(qb_s[..., :MLA_NOPE] * gk[:MLA_NOPE], (1, 0, 2))
    wukt = jnp.transpose(w_uk[layer], (1, 2, 0))
    q_abs = _bmm(qn, wukt, split=True) * (MLA_SCALE * LOG2E)
    q_abs = jnp.pad(jnp.transpose(q_abs, (1, 0, 2)), ((0, 0), (0, 16 - MLA_HEADS), (0, 0))).astype(BF16)
    q_rope = qb_s[..., MLA_NOPE:] * gk[MLA_NOPE:] * (MLA_SCALE * LOG2E)
    q_rope = jnp.pad(q_rope, ((0, 0), (0, 16 - MLA_HEADS), (0, 0))).astype(BF16)
    wukt_flat = wukt.reshape(MLA_HEADS * MLA_NOPE, MLA_KV_LORA).astype(BF16)

    dist_last = PAGE_SIZE - jnp.arange(PAGE_SIZE, dtype=jnp.int32)
    bl = jnp.transpose(tbl[dist_last], (1, 0))
    bl = jnp.repeat(bl.reshape(DA_KV_HEADS, DA_GROUP, 1, PAGE_SIZE), 2, axis=2)
    bl = bl.reshape(DA_KV_HEADS, 2 * DA_GROUP, PAGE_SIZE)
    bias_last = jnp.pad(bl, ((0, 0), (0, 16 - 2 * DA_GROUP), (PAGE_SIZE, 0)))

    s_own_a = jnp.einsum('bngmd,bnmd->bngm', qa_s, ka_s) * (DA_SCALE * LOG2E)
    s_own_a = s_own_a + tbl[0].reshape(1, DA_KV_HEADS, DA_GROUP, 1)
    s_own_a = jnp.pad(s_own_a.reshape(B, DA_KV_HEADS, 2 * DA_GROUP), ((0, 0), (0, 0), (0, 16 - 2 * DA_GROUP)))
    s_own_a = jnp.broadcast_to(s_own_a[..., None], (B, DA_KV_HEADS, 16, 128))
    v_own_a = va_s.reshape(B, DA_KV_HEADS, 1, 2 * DA_D)
    kn_s = _mm(ckv_s.astype(BF16), w_uk_flat, tm=B).reshape(B, MLA_HEADS, MLA_NOPE)
    kb_s = jnp.concatenate(
        [kn_s, jnp.broadcast_to(kpe_s[:, None, :], (B, MLA_HEADS, MLA_ROPE))], axis=-1)
    kb_s = _rms(kb_s, gk)
    s_own_b = jnp.sum(qb_s * kb_s, axis=-1) * (MLA_SCALE * LOG2E)
    s_own_b = jnp.pad(s_own_b, ((0, 0), (0, 16 - MLA_HEADS)))
    s_own_b = jnp.broadcast_to(s_own_b[..., None], (B, 16, 128))
    c_own = ckv_s.reshape(B, 1, MLA_KV_LORA)

    n_phys = cache_da_k.shape[1]
    ra, rb = _decode_attention(
        page_table, qa_mat, q_abs, q_rope, wukt_flat, bias_last,
        s_own_a, v_own_a, s_own_b, c_own,
        jnp.transpose(cache_da_k[layer], (0, 2, 3, 4, 1)).reshape(n_phys, DA_KV_HEADS * 2 * DA_D, PAGE_SIZE),
        cache_da_v[layer].reshape(n_phys, PAGE_SIZE * DA_KV_HEADS, 2 * DA_D),
        cache_mla_ckv[layer],
        jnp.transpose(cache_mla_kpe[layer], (0, 2, 1)))
    ra = ra[:, :, :2 * DA_GROUP].reshape(B, DA_HEADS, 2, 2 * DA_D)
    oa_s = ra[:, :, 0] - lam * ra[:, :, 1]
    rb_h = jnp.transpose(rb[:, :MLA_HEADS], (1, 0, 2)).astype(BF16)
    wuv_h = jnp.transpose(w_uv[layer], (1, 0, 2))
    ob_s = jnp.transpose(_bmm(rb_h, wuv_h, tm=B), (1, 0, 2)).reshape(B, MLA_HEADS * MLA_V)

    def post_attention(x, oa, ob, g_a, g_b, rows):
        oa = (_rms(oa, g_oa[layer]) * (1.0 - lam_init)).reshape(x.shape[0], DA_HEADS * 2 * DA_D)
        ya = _mm(oa.astype(BF16), w_oa[layer])
        yb = _mm(ob.astype(BF16), w_ob[layer])
        mix = (jax.nn.sigmoid(g_a) * ya + jax.nn.sigmoid(g_b) * yb).astype(BF16)
        x1 = x + gt1[rows] * _mm(mix, w_out[layer])
        h2 = _rms(x1, g_norm2[layer]) * (1.0 + sc2[rows]) + sh2[rows]
        return x1, h2

    p_rows, s_rows = slice(0, 1), slice(1, None)
    x1_p, h2_p = post_attention(xp, oa_p, ob_p, ga_p, gb_p, p_rows)
    x1_s, h2_s = post_attention(xs, oa_s, ob_s, ga_s, gb_s, s_rows)
    y_moe = _moe([h2_p, h2_s], w_router[layer], b_router[layer], w_gu[layer], b_gu[layer],
                 w_down[layer], b_down[layer])
    y_prompt = (x1_p + gt2[p_rows] * y_moe[:S])[None]
    y_sample = (x1_s + gt2[s_rows] * y_moe[S:])[:, None]
    return (y_prompt, y_sample,
            ka_p[None, None], va_p[None, None], ckv_p[None, None], kpe_p[None, None],
            ka_s[None, :, None], va_s[None, :, None], ckv_s[None, :, None], kpe_s[None, :, None])
```
